```python
import math
import jax, jax.numpy as jnp
from jax import lax
import numpy as np

D_MODEL = 2048
BATCH = 4
SEQ = 2048
DEPTH = 4
DEC_BATCH = 32
DEC_SEQ = 1
PAST_LEN = 16384
PAGE_SIZE = 128

MIX_WIDTH = D_MODEL
ATTN_WIDTH = MIX_WIDTH // 2
SSM_WIDTH = MIX_WIDTH - ATTN_WIDTH
HEAD_DIM = 64
N_HEADS = ATTN_WIDTH // HEAD_DIM
N_KV_HEADS = max(1, N_HEADS // 8)
KV_REP = N_HEADS // N_KV_HEADS
WINDOW = 128
ROPE_THETA = 10000.0
SSM_GROUP = 16
SSM_GROUPS = SSM_WIDTH // SSM_GROUP
SSM_STATE = 64
D_FF = ((8 * D_MODEL // 3 + 255) // 256) * 256
CONV_W = 3
RMS_EPS = 1e-6
KV_COLS = N_KV_HEADS * HEAD_DIM
IN_COLS = ATTN_WIDTH + 2 * KV_COLS + SSM_WIDTH
NEG_INF = -1e30

kernel_name = "hymba_swa_s5_convffn_decode_step"


def rmsnorm(x, g):
    xf = x.astype(jnp.float32)
    y = xf * lax.rsqrt(jnp.mean(xf * xf, axis=-1, keepdims=True) + RMS_EPS)
    return (y * g.astype(jnp.float32)).astype(x.dtype)


def rope(x, pos):
    half = HEAD_DIM // 2
    inv = ROPE_THETA ** (-jnp.arange(half, dtype=jnp.float32) / half)
    ang = pos.astype(jnp.float32)[:, None] * inv[None, :]
    cos = jnp.cos(ang)[None, :, None, :]
    sin = jnp.sin(ang)[None, :, None, :]
    xf = x.astype(jnp.float32)
    x1, x2 = xf[..., :half], xf[..., half:]
    out = jnp.concatenate([x1 * cos - x2 * sin, x2 * cos + x1 * sin], axis=-1)
    return out.astype(x.dtype)


def sink_attention(q, k, v, mask, sinks):
    s = jnp.einsum("...qhrd,...khd->...hrqk", q.astype(jnp.float32), k.astype(jnp.float32)) * (HEAD_DIM ** -0.5)
    s = jnp.where(mask, s, NEG_INF)
    sink = sinks.astype(jnp.float32).reshape(N_KV_HEADS, KV_REP, 1, 1)
    m = jnp.maximum(jnp.max(s, axis=-1, keepdims=True), sink)
    p = jnp.exp(s - m)
    denom = jnp.sum(p, axis=-1, keepdims=True) + jnp.exp(sink - m)
    o = jnp.einsum("...hrqk,...khd->...qhrd", p / denom, v.astype(jnp.float32))
    return o.astype(q.dtype)


def attn_prompt(q, k, v, sinks):
    B, L = q.shape[:2]
    nb = L // WINDOW
    qb = q.reshape(B, nb, WINDOW, N_KV_HEADS, KV_REP, HEAD_DIM)

    def band(t):
        tb = t.reshape(B, nb, WINDOW, N_KV_HEADS, HEAD_DIM)
        prev = jnp.concatenate([jnp.zeros_like(tb[:, :1]), tb[:, :-1]], axis=1)
        return jnp.concatenate([prev, tb], axis=2)

    qi = jnp.arange(WINDOW)[:, None]
    kj = jnp.arange(2 * WINDOW)[None, :]
    diff = qi + WINDOW - kj
    blk = jnp.arange(nb)[:, None, None]
    valid = (diff >= 0) & (diff <= WINDOW) & (blk * WINDOW - WINDOW + kj >= 0)
    mask = valid[None, :, None, None]
    o = sink_attention(qb, band(k), band(v), mask, sinks)
    return o.reshape(B, L, ATTN_WIDTH)


def attn_sample(q, k, v, buf_k, buf_v, sinks):
    DB, S = q.shape[:2]
    WB = buf_k.shape[1]
    keys = jnp.concatenate([buf_k.astype(k.dtype), k], axis=1)
    vals = jnp.concatenate([buf_v.astype(v.dtype), v], axis=1)
    t = PAST_LEN + jnp.arange(S)
    spos = jnp.concatenate([PAST_LEN - WB + jnp.arange(WB), PAST_LEN + jnp.arange(S)])
    diff = t[:, None] - spos[None, :]
    mask = ((diff >= 0) & (diff <= WINDOW))[None, None, None]
    o = sink_attention(q.reshape(DB, S, N_KV_HEADS, KV_REP, HEAD_DIM), keys, vals, mask, sinks)
    return o.reshape(DB, S, ATTN_WIDTH), keys[:, -WB:], vals[:, -WB:]


def cmul(ar, ai, br, bi):
    return ar * br - ai * bi, ar * bi + ai * br


def s5_mixer(u, a_re, a_im, b_re, b_im, c_re, c_im, d, log_dt, w_glu, b_glu, h0=None):
    f32 = jnp.float32
    Bt, L = u.shape[:2]
    uf = u.astype(f32).reshape(Bt, L, SSM_GROUPS, SSM_GROUP)
    dt = jnp.exp(log_dt.astype(f32))[:, None]
    ar, ai = a_re.astype(f32), a_im.astype(f32)
    mag = jnp.exp(ar * dt)
    lr, li = mag * jnp.cos(ai * dt), mag * jnp.sin(ai * dt)
    nr, ni = lr - 1.0, li
    den = ar * ar + ai * ai
    qr = (nr * ar + ni * ai) / den
    qi = (ni * ar - nr * ai) / den
    bbr, bbi = cmul(qr[..., None], qi[..., None], b_re.astype(f32), b_im.astype(f32))
    xr = jnp.einsum("blgc,gpc->blgp", uf, bbr)
    xi = jnp.einsum("blgc,gpc->blgp", uf, bbi)
    if h0 is not None:
        pr, pi_ = cmul(lr, li, h0[0].astype(f32), h0[1].astype(f32))
        xr = xr.at[:, 0].add(pr)
        xi = xi.at[:, 0].add(pi_)
    a_r = jnp.broadcast_to(lr, (1, L) + lr.shape)
    a_i = jnp.broadcast_to(li, (1, L) + li.shape)

    def combine(e1, e2):
        a1r, a1i, b1r, b1i = e1
        a2r, a2i, b2r, b2i = e2
        nar, nai = cmul(a2r, a2i, a1r, a1i)
        nbr, nbi = cmul(a2r, a2i, b1r, b1i)
        return nar, nai, nbr + b2r, nbi + b2i

    _, _, hr, hi = lax.associative_scan(combine, (a_r, a_i, xr, xi), axis=1)
    y = (jnp.einsum("blgp,gcp->blgc", hr, c_re.astype(f32))
         - jnp.einsum("blgp,gcp->blgc", hi, c_im.astype(f32))
         + d.astype(f32).reshape(SSM_GROUPS, SSM_GROUP) * uf)
    z = jax.nn.gelu(y.reshape(Bt, L, SSM_WIDTH)).astype(u.dtype)
    out = z * jax.nn.sigmoid(z @ w_glu + b_glu)
    return out, hr[:, -1], hi[:, -1]


def conv_ffn(h, w_up, conv_w, conv_b, w_down, buf=None):
    up = h @ w_up
    Bt, L = up.shape[:2]
    if buf is None:
        pad = jnp.zeros((Bt, CONV_W - 1, up.shape[-1]), up.dtype)
    else:
        pad = buf.astype(up.dtype)
    ext = jnp.concatenate([pad, up], axis=1)
    conv = conv_b
    for j in range(CONV_W):
        conv = conv + conv_w[j] * ext[:, j:j + L]
    g, val = jnp.split(conv, 2, axis=-1)
    out = (jax.nn.silu(g) * val) @ w_down
    return out, ext[:, -(CONV_W - 1):]


def trunk_layer(x, pos, p, l, cache=None):
    Bt, L = x.shape[:2]
    h = rmsnorm(x, p["attn_norm_g"][l])
    proj = h @ p["w_in"][l]
    q, k, v, u = jnp.split(proj, [ATTN_WIDTH, ATTN_WIDTH + KV_COLS, ATTN_WIDTH + 2 * KV_COLS], axis=-1)
    q = rope(q.reshape(Bt, L, N_HEADS, HEAD_DIM), pos)
    k = rope(k.reshape(Bt, L, N_KV_HEADS, HEAD_DIM), pos)
    v = v.reshape(Bt, L, N_KV_HEADS, HEAD_DIM)
    sinks = p["attn_sinks"][l]
    if cache is None:
        a = attn_prompt(q, k, v, sinks)
        wb = min(WINDOW, L)
        nk, nv = k[:, -wb:], v[:, -wb:]
        h0, cbuf = None, None
    else:
        buf_k, buf_v, h0r, h0i, cbuf = cache
        a, nk, nv = attn_sample(q, k, v, buf_k, buf_v, sinks)
        h0 = (h0r, h0i)
    s, hr, hi = s5_mixer(u, p["ssm_a_re"][l], p["ssm_a_im"][l], p["ssm_b_re"][l], p["ssm_b_im"][l],
                         p["ssm_c_re"][l], p["ssm_c_im"][l], p["ssm_d"][l], p["ssm_log_dt"][l],
                         p["w_glu"][l], p["b_glu"][l], h0)
    mix = jnp.concatenate([rmsnorm(a, p["attn_out_norm_g"][l]), rmsnorm(s, p["ssm_out_norm_g"][l])], axis=-1)
    x = x + mix @ p["w_out"][l]
    f, nconv = conv_ffn(rmsnorm(x, p["ffn_norm_g"][l]), p["w_up"][l], p["conv_w"][l], p["conv_b"][l],
                        p["w_down"][l], cbuf)
    x = x + f
    return x, (nk, nv, hr, hi, nconv)


def setup_inputs(seed: int = 0) -> dict:
    key = jax.random.key(seed)
    ks = jax.random.split(key, 32)
    f32 = jnp.float32

    def nrm(k, shape, scale):
        return jax.random.normal(k, shape, f32) * scale

    w_buf = min(WINDOW, PAST_LEN)
    n = jnp.arange(SSM_STATE, dtype=f32)
    G, P = SSM_GROUPS, SSM_STATE
    return {
        "x_prompt": nrm(ks[0], (BATCH, SEQ, D_MODEL), 1.0),
        "x_sample": nrm(ks[1], (DEC_BATCH, DEC_SEQ, D_MODEL), 1.0),
        "cache_k": nrm(ks[2], (DEPTH, DEC_BATCH, w_buf, N_KV_HEADS, HEAD_DIM), 1.0),
        "cache_v": nrm(ks[3], (DEPTH, DEC_BATCH, w_buf, N_KV_HEADS, HEAD_DIM), 1.0),
        "state_ssm_re": nrm(ks[4], (DEPTH, DEC_BATCH, G, P), 0.5),
        "state_ssm_im": nrm(ks[5], (DEPTH, DEC_BATCH, G, P), 0.5),
        "state_conv": nrm(ks[6], (DEPTH, DEC_BATCH, CONV_W - 1, 2 * D_FF), 1.0),
        "attn_norm_g": 1.0 + nrm(ks[7], (DEPTH, D_MODEL), 0.02),
        "w_in": nrm(ks[8], (DEPTH, D_MODEL, IN_COLS), D_MODEL ** -0.5),
        "attn_sinks": nrm(ks[9], (DEPTH, N_HEADS), 0.5),
        "ssm_a_re": -0.5 + nrm(ks[10], (DEPTH, G, P), 0.01),
        "ssm_a_im": math.pi * n + nrm(ks[11], (DEPTH, G, P), 0.01),
        "ssm_b_re": nrm(ks[12], (DEPTH, G, P, SSM_GROUP), (2 * SSM_GROUP) ** -0.5),
        "ssm_b_im": nrm(ks[13], (DEPTH, G, P, SSM_GROUP), (2 * SSM_GROUP) ** -0.5),
        "ssm_c_re": nrm(ks[14], (DEPTH, G, SSM_GROUP, P), (2 * P) ** -0.5),
        "ssm_c_im": nrm(ks[15], (DEPTH, G, SSM_GROUP, P), (2 * P) ** -0.5),
        "ssm_d": nrm(ks[16], (DEPTH, SSM_WIDTH), 0.5),
        "ssm_log_dt": jax.random.uniform(ks[17], (DEPTH, G), f32, math.log(1e-3), math.log(1e-1)),
        "w_glu": nrm(ks[18], (DEPTH, SSM_WIDTH, SSM_WIDTH), SSM_WIDTH ** -0.5),
        "b_glu": nrm(ks[19], (DEPTH, SSM_WIDTH), 0.01),
        "attn_out_norm_g": 1.0 + nrm(ks[20], (DEPTH, ATTN_WIDTH), 0.02),
        "ssm_out_norm_g": 1.0 + nrm(ks[21], (DEPTH, SSM_WIDTH), 0.02),
        "w_out": nrm(ks[22], (DEPTH, MIX_WIDTH, D_MODEL), MIX_WIDTH ** -0.5),
        "ffn_norm_g": 1.0 + nrm(ks[23], (DEPTH, D_MODEL), 0.02),
        "w_up": nrm(ks[24], (DEPTH, D_MODEL, 2 * D_FF), D_MODEL ** -0.5),
        "conv_w": nrm(ks[25], (DEPTH, CONV_W, 2 * D_FF), CONV_W ** -0.5),
        "conv_b": nrm(ks[26], (DEPTH, 2 * D_FF), 0.01),
        "w_down": nrm(ks[27], (DEPTH, D_FF, D_MODEL), D_FF ** -0.5),
        "final_norm_g": 1.0 + nrm(ks[28], (D_MODEL,), 0.02),
    }


def reference(x_prompt, x_sample, cache_k, cache_v, state_ssm_re, state_ssm_im, state_conv,
              attn_norm_g, w_in, attn_sinks, ssm_a_re, ssm_a_im, ssm_b_re, ssm_b_im, ssm_c_re, ssm_c_im,
              ssm_d, ssm_log_dt, w_glu, b_glu, attn_out_norm_g, ssm_out_norm_g, w_out, ffn_norm_g,
              w_up, conv_w, conv_b, w_down, final_norm_g):
    p = dict(attn_norm_g=attn_norm_g, w_in=w_in, attn_sinks=attn_sinks, ssm_a_re=ssm_a_re, ssm_a_im=ssm_a_im,
             ssm_b_re=ssm_b_re, ssm_b_im=ssm_b_im, ssm_c_re=ssm_c_re, ssm_c_im=ssm_c_im, ssm_d=ssm_d,
             ssm_log_dt=ssm_log_dt, w_glu=w_glu, b_glu=b_glu, attn_out_norm_g=attn_out_norm_g,
             ssm_out_norm_g=ssm_out_norm_g, w_out=w_out, ffn_norm_g=ffn_norm_g, w_up=w_up,
             conv_w=conv_w, conv_b=conv_b, w_down=w_down)
    pos_p = jnp.arange(x_prompt.shape[1], dtype=jnp.int32)
    pos_s = PAST_LEN + jnp.arange(x_sample.shape[1], dtype=jnp.int32)
    xp, xs = x_prompt, x_sample
    st_p = []
    st_s = []
    for l in range(DEPTH):
        xp, sp = trunk_layer(xp, pos_p, p, l)
        xs, ss = trunk_layer(xs, pos_s, p, l,
                             (cache_k[l], cache_v[l], state_ssm_re[l], state_ssm_im[l], state_conv[l]))
        st_p.append(sp)
        st_s.append(ss)
    y_prompt = rmsnorm(xp, final_norm_g)
    y_sample = rmsnorm(xs, final_norm_g)
    k_prompt = jnp.stack([s[0] for s in st_p], 0)
    v_prompt = jnp.stack([s[1] for s in st_p], 0)
    ssm_re_prompt = jnp.stack([s[2] for s in st_p], 0)
    ssm_im_prompt = jnp.stack([s[3] for s in st_p], 0)
    conv_prompt = jnp.stack([s[4] for s in st_p], 0)
    k_sample = jnp.stack([s[0] for s in st_s], 0)
    v_sample = jnp.stack([s[1] for s in st_s], 0)
    ssm_re_sample = jnp.stack([s[2] for s in st_s], 0)
    ssm_im_sample = jnp.stack([s[3] for s in st_s], 0)
    conv_sample = jnp.stack([s[4] for s in st_s], 0)
    return (y_prompt, y_sample, k_prompt, v_prompt, ssm_re_prompt, ssm_im_prompt, conv_prompt,
            k_sample, v_sample, ssm_re_sample, ssm_im_sample, conv_sample)
```

```python
import functools
import math

import jax
import jax.numpy as jnp
from jax import lax
from jax.experimental import pallas as pl
from jax.experimental.pallas import tpu as pltpu

F32 = jnp.float32
BF16 = jnp.bfloat16

D_MODEL = 2048
DEPTH = 4
ATTN_WIDTH = 1024
SSM_WIDTH = 1024
HEAD_DIM = 64
N_HEADS = 16
N_KV_HEADS = 2
KV_REP = 8
KV_COLS = N_KV_HEADS * HEAD_DIM
WINDOW = 128
ROPE_THETA = 10000.0
SSM_GROUP = 16
SSM_GROUPS = 64
SSM_STATE = 64
N_STATE = SSM_GROUPS * SSM_STATE
D_FF = 5632
CONV_W = 3
RMS_EPS = 1e-6
IN_COLS = ATTN_WIDTH + 2 * KV_COLS + SSM_WIDTH
NEG_INF = -1e30
PAST_LEN = 16384

LANES = 128
SUBLANES = 8
MIB = 1024 * 1024
U_BLOCKS = SSM_WIDTH // LANES

GROUP_CHUNK = 16
N_CHUNK = SSM_GROUPS // GROUP_CHUNK
CH_U = GROUP_CHUNK * SSM_GROUP
CH_S = GROUP_CHUNK * SSM_STATE

SEG_LEN = 32
SSM_TILE = SUBLANES * SEG_LEN
SCAN_LANES = 512


def _rms(x, g):
    ms = jnp.mean(x * x, axis=-1, keepdims=True)
    return x * lax.rsqrt(ms + RMS_EPS) * g


def _params(sem, vmem_mib):
    return pltpu.CompilerParams(dimension_semantics=sem, vmem_limit_bytes=vmem_mib * MIB)


def _resident(shape):
    nd = len(shape)
    return pl.BlockSpec(shape, lambda *_: (0,) * nd, pipeline_mode=pl.Buffered(1))


def _in_proj_kernel(x_ref, g_ref, w_ref, cos_ref, sin_ref, q_ref, k_ref, v_ref, u_ref):
    h = _rms(x_ref[...], g_ref[...]).astype(BF16)
    proj = jnp.dot(h, w_ref[...], preferred_element_type=F32)
    cos = cos_ref[...]
    sin = sin_ref[...]
    lane = lax.broadcasted_iota(jnp.int32, cos.shape, 1)
    first_half = (lane & (HEAD_DIM - 1)) < (HEAD_DIM // 2)

    def rope(blk):
        partner = jnp.where(first_half,
                            pltpu.roll(blk, LANES - HEAD_DIM // 2, 1),
                            pltpu.roll(blk, HEAD_DIM // 2, 1))
        return blk * cos + partner * sin

    for j in range(ATTN_WIDTH // LANES):
        q_ref[:, j * LANES:(j + 1) * LANES] = rope(proj[:, j * LANES:(j + 1) * LANES]).astype(BF16)
    k_ref[...] = rope(proj[:, ATTN_WIDTH:ATTN_WIDTH + KV_COLS])
    v_ref[...] = proj[:, ATTN_WIDTH + KV_COLS:ATTN_WIDTH + 2 * KV_COLS]
    u0 = ATTN_WIDTH + 2 * KV_COLS
    for j in range(U_BLOCKS):
        u_ref[j] = proj[:, u0 + j * LANES:u0 + (j + 1) * LANES]


def _in_proj(x, g, w, cos, sin, tm):
    m = x.shape[0]
    pos_tiles = cos.shape[0] // tm
    row = lambda i: (i, 0)
    return pl.pallas_call(
        _in_proj_kernel,
        grid=(m // tm,),
        in_specs=[pl.BlockSpec((tm, D_MODEL), row),
                  _resident((1, D_MODEL)),
                  _resident((D_MODEL, IN_COLS)),
                  pl.BlockSpec((tm, LANES), lambda i: (i % pos_tiles, 0)),
                  pl.BlockSpec((tm, LANES), lambda i: (i % pos_tiles, 0))],
        out_specs=[pl.BlockSpec((tm, ATTN_WIDTH), row),
                   pl.BlockSpec((tm, KV_COLS), row),
                   pl.BlockSpec((tm, KV_COLS), row),
                   pl.BlockSpec((U_BLOCKS, tm, LANES), lambda i: (0, i, 0))],
        out_shape=[jax.ShapeDtypeStruct((m, ATTN_WIDTH), BF16),
                   jax.ShapeDtypeStruct((m, KV_COLS), F32),
                   jax.ShapeDtypeStruct((m, KV_COLS), F32),
                   jax.ShapeDtypeStruct((U_BLOCKS, m, LANES), F32)],
        compiler_params=_params(("arbitrary",), 48),
        name="in_proj",
    )(x, g, w, cos, sin)


def _attn_prompt_kernel(sinks_ref, q_ref, kc_ref, kp_ref, vc_ref, vp_ref, gn_ref, a_ref):
    j = pl.program_id(1)
    lane = lax.broadcasted_iota(jnp.int32, (WINDOW, LANES), 1)
    lo = lane < HEAD_DIM
    qi = lax.broadcasted_iota(jnp.int32, (WINDOW, 2 * WINDOW), 0)
    kj = lax.broadcasted_iota(jnp.int32, (WINDOW, 2 * WINDOW), 1)
    diff = qi + WINDOW - kj
    first_key = jnp.where(j > 0, 0, WINDOW)
    valid = (diff >= 0) & (diff <= WINDOW) & (kj >= first_key)
    zero = jnp.zeros((WINDOW, LANES), F32)

    def halves(x, g):
        r = pltpu.roll(x, HEAD_DIM, 1)
        if g == 0:
            return jnp.where(lo, x, zero), jnp.where(lo, zero, r)
        return jnp.where(lo, r, zero), jnp.where(lo, zero, x)

    kp, kc, vp, vc = kp_ref[...], kc_ref[...], vp_ref[...], vc_ref[...]
    scale = HEAD_DIM ** -0.5
    pairs = KV_REP // 2
    out_blocks = []
    for g in range(N_KV_HEADS):
        kp_lo, kp_hi = halves(kp, g)
        kc_lo, kc_hi = halves(kc, g)
        vp_lo, vp_hi = halves(vp, g)
        vc_lo, vc_hi = halves(vc, g)
        k_lo = jnp.concatenate([kp_lo, kc_lo], axis=0).astype(BF16)
        k_hi = jnp.concatenate([kp_hi, kc_hi], axis=0).astype(BF16)
        v_lo = jnp.concatenate([vp_lo, vc_lo], axis=0).astype(BF16)
        v_hi = jnp.concatenate([vp_hi, vc_hi], axis=0).astype(BF16)
        qg = jnp.concatenate([q_ref[:, (g * pairs + p) * LANES:(g * pairs + p + 1) * LANES]
                              for p in range(pairs)], axis=0)
        nt = (((1,), (1,)), ((), ()))
        s_even = lax.dot_general(qg, k_lo, nt, preferred_element_type=F32) * scale
        s_odd = lax.dot_general(qg, k_hi, nt, preferred_element_type=F32) * scale

        def softmax(s, head):
            s = jnp.where(valid, s, NEG_INF)
            sink = sinks_ref[head]
            m = jnp.maximum(jnp.max(s, axis=-1, keepdims=True), sink)
            p = jnp.exp(s - m)
            denom = jnp.sum(p, axis=-1, keepdims=True) + jnp.exp(sink - m)
            return (p / denom).astype(BF16)

        p_even = jnp.concatenate(
            [softmax(s_even[p * WINDOW:(p + 1) * WINDOW], g * KV_REP + 2 * p) for p in range(pairs)], axis=0)
        p_odd = jnp.concatenate(
            [softmax(s_odd[p * WINDOW:(p + 1) * WINDOW], g * KV_REP + 2 * p + 1) for p in range(pairs)], axis=0)
        o = (jnp.dot(p_even, v_lo, preferred_element_type=F32)
             + jnp.dot(p_odd, v_hi, preferred_element_type=F32))
        out_blocks += [o[p * WINDOW:(p + 1) * WINDOW] for p in range(pairs)]

    ssq = out_blocks[0] * out_blocks[0]
    for blk in out_blocks[1:]:
        ssq = ssq + blk * blk
    inv = lax.rsqrt(jnp.sum(ssq, axis=-1, keepdims=True) / ATTN_WIDTH + RMS_EPS)
    for i, blk in enumerate(out_blocks):
        a_ref[:, i * LANES:(i + 1) * LANES] = (blk * inv * gn_ref[:, i * LANES:(i + 1) * LANES]).astype(BF16)


def _attn_prompt(q, k, v, sinks, gn, batch, seq):
    nb = seq // WINDOW
    cur = lambda b, j: (b * nb + j, 0)
    prev = lambda b, j: (b * nb + jnp.maximum(j - 1, 0), 0)
    kv_spec = lambda im: pl.BlockSpec((WINDOW, KV_COLS), im)
    return pl.pallas_call(
        _attn_prompt_kernel,
        grid=(batch, nb),
        in_specs=[pl.BlockSpec(memory_space=pltpu.SMEM),
                  pl.BlockSpec((WINDOW, ATTN_WIDTH), cur),
                  kv_spec(cur), kv_spec(prev), kv_spec(cur), kv_spec(prev),
                  _resident((1, ATTN_WIDTH))],
        out_specs=pl.BlockSpec((WINDOW, ATTN_WIDTH), cur),
        out_shape=jax.ShapeDtypeStruct((batch * seq, ATTN_WIDTH), BF16),
        compiler_params=_params(("arbitrary", "arbitrary"), 32),
        name="attn_prompt",
    )(sinks, q, k, k, v, v, gn)


def _attn_sample_kernel(qbd_ref, kc_ref, vc_ref, kn_ref, vn_ref, sink_ref, o_ref):
    scale = HEAD_DIM ** -0.5
    sink = sink_ref[...]
    nt = (((1,), (1,)), ((), ()))
    for i in range(qbd_ref.shape[0]):
        qb = qbd_ref[i]
        kn, vn = kn_ref[i], vn_ref[i]
        s = lax.dot_general(qb, kc_ref[i].astype(BF16), nt, preferred_element_type=F32) * scale
        s_new = jnp.sum(qb.astype(F32) * kn, axis=-1, keepdims=True) * scale
        m = jnp.maximum(jnp.maximum(jnp.max(s, axis=-1, keepdims=True), s_new), sink)
        p = jnp.exp(s - m)
        p_new = jnp.exp(s_new - m)
        denom = jnp.sum(p, axis=-1, keepdims=True) + p_new + jnp.exp(sink - m)
        o_ref[i] = (jnp.dot((p / denom).astype(BF16), vc_ref[i].astype(BF16), preferred_element_type=F32)
                    + (p_new / denom) * vn)


def _attn_sample(qbd, kc, vc, kn, vn, sink_col, bb):
    nbatch = qbd.shape[0]
    blk = lambda *s: pl.BlockSpec((bb,) + s, lambda i: (i, 0, 0))
    return pl.pallas_call(
        _attn_sample_kernel,
        grid=(nbatch // bb,),
        in_specs=[blk(N_HEADS, LANES), blk(WINDOW, KV_COLS), blk(WINDOW, KV_COLS),
                  blk(1, KV_COLS), blk(1, KV_COLS), _resident((N_HEADS, 1))],
        out_specs=blk(N_HEADS, LANES),
        out_shape=jax.ShapeDtypeStruct((nbatch, N_HEADS, LANES), F32),
        compiler_params=_params(("arbitrary",), 32),
        name="attn_sample",
    )(qbd, kc, vc, kn, vn, sink_col)


def _rms_cast_kernel(x_ref, g_ref, o_ref):
    o_ref[...] = _rms(x_ref[...], g_ref[...]).astype(BF16)


def _rms_cast(x, g):
    return pl.pallas_call(
        _rms_cast_kernel,
        out_shape=jax.ShapeDtypeStruct(x.shape, BF16),
        name="rms_cast",
    )(x, g)


def _glu_norm(z, wg_ref, bg_ref, gn_ref):
    gate = jax.nn.sigmoid(jnp.dot(z.astype(BF16), wg_ref[...], preferred_element_type=F32) + bg_ref[...])
    return _rms(z * gate, gn_ref[...])


def _ssm_prompt_kernel(u_ref, bre_ref, bim_ref, cre_ref, cimn_ref, lre_ref, lim_ref, d_ref, wg_ref, bg_ref, gn_ref,
                       s_ref, hre_ref, him_ref,
                       up_s, xre_s, xim_s, hbre_s, hbim_s, pre_s, pim_s, cre_s, cim_s, hinre_s, hinim_s, z_s):
    b = pl.program_id(0)
    j = pl.program_id(1)

    @pl.when((b == 0) & (j == 0))
    def _():
        lr, li = lre_ref[...], lim_ref[...]
        pr, pi = lr, li
        pre_s[0:1, :] = pr
        pim_s[0:1, :] = pi
        for i in range(1, SEG_LEN):
            pr, pi = pr * lr - pi * li, pr * li + pi * lr
            pre_s[i:i + 1, :] = pr
            pim_s[i:i + 1, :] = pi

    @pl.when(j == 0)
    def _():
        cre_s[...] = jnp.zeros_like(cre_s)
        cim_s[...] = jnp.zeros_like(cim_s)

    for i in range(SEG_LEN):
        for jb in range(U_BLOCKS):
            up_s[jb, i * SUBLANES:(i + 1) * SUBLANES, :] = u_ref[jb, pl.ds(i, SUBLANES, stride=SEG_LEN), :]

    blocks_per_chunk = CH_U // LANES
    for c in range(N_CHUNK):
        uc = jnp.concatenate([up_s[c * blocks_per_chunk + h] for h in range(blocks_per_chunk)], axis=1).astype(BF16)
        xre_s[:, c * CH_S:(c + 1) * CH_S] = jnp.dot(uc, bre_ref[c], preferred_element_type=F32)
        xim_s[:, c * CH_S:(c + 1) * CH_S] = jnp.dot(uc, bim_ref[c], preferred_element_type=F32)

    for lc in range(N_STATE // SCAN_LANES):
        sl = slice(lc * SCAN_LANES, (lc + 1) * SCAN_LANES)
        lr = jnp.broadcast_to(lre_ref[:, sl], (SUBLANES, SCAN_LANES))
        li = jnp.broadcast_to(lim_ref[:, sl], (SUBLANES, SCAN_LANES))

        def step(i, carry, sl=sl, lr=lr, li=li):
            hr, hi = carry
            r0 = pl.multiple_of(i * SUBLANES, SUBLANES)
            nr = (lr * hr - li * hi) + xre_s[pl.ds(r0, SUBLANES), sl]
            ni = (lr * hi + li * hr) + xim_s[pl.ds(r0, SUBLANES), sl]
            xre_s[pl.ds(r0, SUBLANES), sl] = nr
            xim_s[pl.ds(r0, SUBLANES), sl] = ni
            return nr, ni

        zero = jnp.zeros((SUBLANES, SCAN_LANES), F32)
        lax.fori_loop(0, SEG_LEN, step, (zero, zero), unroll=4)

    l32r = pre_s[SEG_LEN - 1:SEG_LEN, :]
    l32i = pim_s[SEG_LEN - 1:SEG_LEN, :]
    cr, ci = cre_s[...], cim_s[...]
    last = (SEG_LEN - 1) * SUBLANES
    for seg in range(SUBLANES):
        hinre_s[seg:seg + 1, :] = cr
        hinim_s[seg:seg + 1, :] = ci
        er = xre_s[last + seg:last + seg + 1, :]
        ei = xim_s[last + seg:last + seg + 1, :]
        cr, ci = (l32r * cr - l32i * ci) + er, (l32r * ci + l32i * cr) + ei
    cre_s[...] = cr
    cim_s[...] = ci

    @pl.when(j == pl.num_programs(1) - 1)
    def _():
        hre_ref[...] = cr
        him_ref[...] = ci

    for lc in range(N_STATE // SCAN_LANES):
        sl = slice(lc * SCAN_LANES, (lc + 1) * SCAN_LANES)
        hr_in = jnp.concatenate([hinre_s[:, sl]] * 2, axis=0)
        hi_in = jnp.concatenate([hinim_s[:, sl]] * 2, axis=0)

        def fix(i2, _, sl=sl, hr_in=hr_in, hi_in=hi_in):
            r0 = pl.multiple_of(i2 * 2 * SUBLANES, 2 * SUBLANES)
            shape = (SUBLANES, SCAN_LANES)
            pr = jnp.concatenate([jnp.broadcast_to(pre_s[pl.ds(2 * i2, 1), sl], shape),
                                  jnp.broadcast_to(pre_s[pl.ds(2 * i2 + 1, 1), sl], shape)], axis=0)
            pi = jnp.concatenate([jnp.broadcast_to(pim_s[pl.ds(2 * i2, 1), sl], shape),
                                  jnp.broadcast_to(pim_s[pl.ds(2 * i2 + 1, 1), sl], shape)], axis=0)
            tr = xre_s[pl.ds(r0, 2 * SUBLANES), sl] + (pr * hr_in - pi * hi_in)
            ti = xim_s[pl.ds(r0, 2 * SUBLANES), sl] + (pr * hi_in + pi * hr_in)
            hbre_s[pl.ds(r0, 2 * SUBLANES), sl] = tr.astype(BF16)
            hbim_s[pl.ds(r0, 2 * SUBLANES), sl] = ti.astype(BF16)
            return 0

        lax.fori_loop(0, SEG_LEN // 2, fix, 0, unroll=2)

    for c in range(N_CHUNK):
        y = (jnp.dot(hbre_s[:, c * CH_S:(c + 1) * CH_S], cre_ref[c], preferred_element_type=F32)
             + jnp.dot(hbim_s[:, c * CH_S:(c + 1) * CH_S], cimn_ref[c], preferred_element_type=F32))
        for h in range(blocks_per_chunk):
            jb = c * blocks_per_chunk + h
            z_s[jb] = jax.nn.gelu(y[:, h * LANES:(h + 1) * LANES] + d_ref[:, jb * LANES:(jb + 1) * LANES] * up_s[jb])

    z = jnp.concatenate([z_s[jb] for jb in range(U_BLOCKS)], axis=1)
    out = _glu_norm(z, wg_ref, bg_ref, gn_ref)
    for jb in range(U_BLOCKS):
        z_s[jb] = out[:, jb * LANES:(jb + 1) * LANES]

    for seg in range(SUBLANES):
        for i0 in range(0, SEG_LEN, 2 * SUBLANES):
            t0 = seg * SEG_LEN + i0
            for jb in range(U_BLOCKS):
                rows = jnp.concatenate(
                    [z_s[jb, pl.ds((i0 + k) * SUBLANES + seg, SUBLANES, stride=SUBLANES), :] for k in (0, SUBLANES)],
                    axis=0)
                s_ref[t0:t0 + 2 * SUBLANES, jb * LANES:(jb + 1) * LANES] = rows.astype(BF16)


def _ssm_prompt(u, ssm, wg, bg, gn, batch, seq):
    nt = seq // SSM_TILE
    row = lambda b, j: (b * nt + j, 0)
    st = lambda b, j: (b, 0, 0)
    scratch = [
        pltpu.VMEM((U_BLOCKS, SSM_TILE, LANES), F32),
        pltpu.VMEM((SSM_TILE, N_STATE), F32),
        pltpu.VMEM((SSM_TILE, N_STATE), F32),
        pltpu.VMEM((SSM_TILE, N_STATE), BF16),
        pltpu.VMEM((SSM_TILE, N_STATE), BF16),
        pltpu.VMEM((SEG_LEN, N_STATE), F32),
        pltpu.VMEM((SEG_LEN, N_STATE), F32),
        pltpu.VMEM((1, N_STATE), F32),
        pltpu.VMEM((1, N_STATE), F32),
        pltpu.VMEM((SUBLANES, N_STATE), F32),
        pltpu.VMEM((SUBLANES, N_STATE), F32),
        pltpu.VMEM((U_BLOCKS, SSM_TILE, LANES), F32),
    ]
    return pl.pallas_call(
        _ssm_prompt_kernel,
        grid=(batch, nt),
        in_specs=[pl.BlockSpec((U_BLOCKS, SSM_TILE, LANES), lambda b, j: (0, b * nt + j, 0)),
                  _resident((N_CHUNK, CH_U, CH_S)), _resident((N_CHUNK, CH_U, CH_S)),
                  _resident((N_CHUNK, CH_S, CH_U)), _resident((N_CHUNK, CH_S, CH_U)),
                  _resident((1, N_STATE)), _resident((1, N_STATE)), _resident((1, SSM_WIDTH)),
                  _resident((SSM_WIDTH, SSM_WIDTH)), _resident((1, SSM_WIDTH)), _resident((1, SSM_WIDTH))],
        out_specs=[pl.BlockSpec((SSM_TILE, SSM_WIDTH), row),
                   pl.BlockSpec((None, 1, N_STATE), st),
                   pl.BlockSpec((None, 1, N_STATE), st)],
        out_shape=[jax.ShapeDtypeStruct((batch * seq, SSM_WIDTH), BF16),
                   jax.ShapeDtypeStruct((batch, 1, N_STATE), F32),
                   jax.ShapeDtypeStruct((batch, 1, N_STATE), F32)],
        scratch_shapes=scratch,
        compiler_params=_params(("arbitrary", "arbitrary"), 48),
        name="ssm_prompt",
    )(u, ssm["bre"], ssm["bim"], ssm["cre"], ssm["cimn"], ssm["lre"], ssm["lim"], ssm["d"], wg, bg, gn)


def _ssm_sample_kernel(u_ref, h0re_ref, h0im_ref, bre_ref, bim_ref, cre_ref, cimn_ref, lre_ref, lim_ref, d_ref,
                       wg_ref, bg_ref, gn_ref, s_ref, hre_ref, him_ref):
    u = jnp.concatenate([u_ref[jb] for jb in range(U_BLOCKS)], axis=1)
    zs = []
    for c in range(N_CHUNK):
        cs = slice(c * CH_U, (c + 1) * CH_U)
        ss = slice(c * CH_S, (c + 1) * CH_S)
        uc = u[:, cs].astype(BF16)
        lr, li = lre_ref[:, ss], lim_ref[:, ss]
        h0r, h0i = h0re_ref[:, ss], h0im_ref[:, ss]
        hr = jnp.dot(uc, bre_ref[c], preferred_element_type=F32) + (lr * h0r - li * h0i)
        hi = jnp.dot(uc, bim_ref[c], preferred_element_type=F32) + (lr * h0i + li * h0r)
        hre_ref[:, ss] = hr
        him_ref[:, ss] = hi
        y = (jnp.dot(hr.astype(BF16), cre_ref[c], preferred_element_type=F32)
             + jnp.dot(hi.astype(BF16), cimn_ref[c], preferred_element_type=F32))
        zs.append(jax.nn.gelu(y + d_ref[:, cs] * u[:, cs]))
    z = jnp.concatenate(zs, axis=1)
    s_ref[...] = _glu_norm(z, wg_ref, bg_ref, gn_ref).astype(BF16)


def _ssm_sample(u, h0re, h0im, ssm, wg, bg, gn):
    n = u.shape[1]
    return pl.pallas_call(
        _ssm_sample_kernel,
        out_shape=[jax.ShapeDtypeStruct((n, SSM_WIDTH), BF16),
                   jax.ShapeDtypeStruct((n, N_STATE), F32),
                   jax.ShapeDtypeStruct((n, N_STATE), F32)],
        compiler_params=pltpu.CompilerParams(vmem_limit_bytes=40 * MIB),
        name="ssm_sample",
    )(u, h0re, h0im, ssm["bre"], ssm["bim"], ssm["cre"], ssm["cimn"], ssm["lre"], ssm["lim"], ssm["d"], wg, bg, gn)


def _out_proj_kernel(x_ref, a_ref, s_ref, w_ref, o_ref):
    acc = (jnp.dot(a_ref[...], w_ref[0:ATTN_WIDTH, :], preferred_element_type=F32)
           + jnp.dot(s_ref[...], w_ref[ATTN_WIDTH:, :], preferred_element_type=F32))
    o_ref[...] = x_ref[...] + acc


def _out_proj(x, a, s, w, tm):
    m = x.shape[0]
    row = lambda i: (i, 0)
    return pl.pallas_call(
        _out_proj_kernel,
        grid=(m // tm,),
        in_specs=[pl.BlockSpec((tm, D_MODEL), row),
                  pl.BlockSpec((tm, ATTN_WIDTH), row),
                  pl.BlockSpec((tm, SSM_WIDTH), row),
                  _resident((D_MODEL, D_MODEL))],
        out_specs=pl.BlockSpec((tm, D_MODEL), row),
        out_shape=jax.ShapeDtypeStruct((m, D_MODEL), F32),
        compiler_params=_params(("arbitrary",), 48),
        name="out_proj",
    )(x, a, s, w)


FF_TILE = 512
N_FF = D_FF // FF_TILE
CONV_PAD = SUBLANES


def _ffn_prompt_kernel(final_norm, tiles_per_seq,
                       x_ref, gn_ref, wug_ref, wuv_ref, cwg_ref, cwv_ref, cbg_ref, cbv_ref, wd_ref, fg_ref,
                       o_ref, cg_ref, cv_ref,
                       h_s, acc_s, extg_s, extv_s, carryg_s, carryv_s):
    m = pl.program_id(0)
    f = pl.program_id(1)
    tm = x_ref.shape[0]

    @pl.when(f == 0)
    def _():
        h_s[...] = _rms(x_ref[...], gn_ref[...]).astype(BF16)
        acc_s[...] = jnp.zeros_like(acc_s)

    seq_start = (m % tiles_per_seq) == 0

    def conv(w_ref, cw_ref, cb_ref, ext_s, carry_s, state_ref):
        up = jnp.dot(h_s[...], w_ref[...], preferred_element_type=F32)

        @pl.when(seq_start)
        def _():
            ext_s[0:CONV_PAD, :] = jnp.zeros((CONV_PAD, FF_TILE), F32)

        @pl.when(jnp.logical_not(seq_start))
        def _():
            ext_s[0:CONV_PAD, :] = carry_s[f]

        ext_s[CONV_PAD:, :] = up
        carry_s[f] = up[tm - CONV_PAD:, :]
        state_ref[f] = up[tm - (CONV_W - 1):, :]
        out = cb_ref[...]
        for t in range(CONV_W):
            lag = CONV_W - 1 - t
            out = out + cw_ref[t:t + 1, :] * ext_s[CONV_PAD - lag:CONV_PAD - lag + tm, :]
        return out

    gate = conv(wug_ref, cwg_ref, cbg_ref, extg_s, carryg_s, cg_ref)
    val = conv(wuv_ref, cwv_ref, cbv_ref, extv_s, carryv_s, cv_ref)
    act = (jax.nn.silu(gate) * val).astype(BF16)
    acc_s[...] += jnp.dot(act, wd_ref[...], preferred_element_type=F32)

    @pl.when(f == pl.num_programs(1) - 1)
    def _():
        y = x_ref[...] + acc_s[...]
        o_ref[...] = _rms(y, fg_ref[...]) if final_norm else y


def _ffn_prompt(x, gn, wu, cw, cb, wd, fg, batch, seq, tm, final_norm):
    m = batch * seq
    tiles_per_seq = seq // tm
    row = lambda i, f: (i, 0)
    gcol = lambda i, f: (0, f)
    vcol = lambda i, f: (0, N_FF + f)
    state = lambda i, f: (i // tiles_per_seq, 0, 0, 0)
    return pl.pallas_call(
        functools.partial(_ffn_prompt_kernel, final_norm, tiles_per_seq),
        grid=(m // tm, N_FF),
        in_specs=[pl.BlockSpec((tm, D_MODEL), row),
                  _resident((1, D_MODEL)),
                  pl.BlockSpec((D_MODEL, FF_TILE), gcol), pl.BlockSpec((D_MODEL, FF_TILE), vcol),
                  pl.BlockSpec((CONV_W, FF_TILE), gcol), pl.BlockSpec((CONV_W, FF_TILE), vcol),
                  pl.BlockSpec((1, FF_TILE), gcol), pl.BlockSpec((1, FF_TILE), vcol),
                  pl.BlockSpec((FF_TILE, D_MODEL), lambda i, f: (f, 0)),
                  _resident((1, D_MODEL))],
        out_specs=[pl.BlockSpec((tm, D_MODEL), row),
                   pl.BlockSpec((None, N_FF, CONV_W - 1, FF_TILE), state),
                   pl.BlockSpec((None, N_FF, CONV_W - 1, FF_TILE), state)],
        out_shape=[jax.ShapeDtypeStruct((m, D_MODEL), F32),
                   jax.ShapeDtypeStruct((batch, N_FF, CONV_W - 1, FF_TILE), F32),
                   jax.ShapeDtypeStruct((batch, N_FF, CONV_W - 1, FF_TILE), F32)],
        scratch_shapes=[pltpu.VMEM((tm, D_MODEL), BF16),
                        pltpu.VMEM((tm, D_MODEL), F32),
                        pltpu.VMEM((tm + CONV_PAD, FF_TILE), F32),
                        pltpu.VMEM((tm + CONV_PAD, FF_TILE), F32),
                        pltpu.VMEM((N_FF, CONV_PAD, FF_TILE), F32),
                        pltpu.VMEM((N_FF, CONV_PAD, FF_TILE), F32)],
        compiler_params=_params(("arbitrary", "arbitrary"), 52),
        name="ffn_prompt",
    )(x, gn, wu, wu, cw, cw, cb, cb, wd, fg)


def _ffn_sample_kernel(final_norm,
                       x_ref, gn_ref, wug_ref, wuv_ref, cwg_ref, cwv_ref, cbg_ref, cbv_ref, wd_ref, fg_ref,
                       s0g_ref, s0v_ref, s1g_ref, s1v_ref,
                       o_ref, ug_ref, uv_ref, h_s, acc_s):
    f = pl.program_id(0)

    @pl.when(f == 0)
    def _():
        h_s[...] = _rms(x_ref[...], gn_ref[...]).astype(BF16)
        acc_s[...] = jnp.zeros_like(acc_s)

    def conv(w_ref, cw_ref, cb_ref, s0_ref, s1_ref, up_ref):
        up = jnp.dot(h_s[...], w_ref[...], preferred_element_type=F32)
        up_ref[...] = up
        return ((cb_ref[...] + cw_ref[0:1, :] * s0_ref[...]) + cw_ref[1:2, :] * s1_ref[...]) + cw_ref[2:3, :] * up

    gate = conv(wug_ref, cwg_ref, cbg_ref, s0g_ref, s1g_ref, ug_ref)
    val = conv(wuv_ref, cwv_ref, cbv_ref, s0v_ref, s1v_ref, uv_ref)
    act = (jax.nn.silu(gate) * val).astype(BF16)
    acc_s[...] += jnp.dot(act, wd_ref[...], preferred_element_type=F32)

    @pl.when(f == pl.num_programs(0) - 1)
    def _():
        y = x_ref[...] + acc_s[...]
        o_ref[...] = _rms(y, fg_ref[...]) if final_norm else y


def _ffn_sample(x, gn, wu, cw, cb, wd, fg, s0, s1, final_norm):
    n = x.shape[0]
    gcol = lambda f: (0, f)
    vcol = lambda f: (0, N_FF + f)
    return pl.pallas_call(
        functools.partial(_ffn_sample_kernel, final_norm),
        grid=(N_FF,),
        in_specs=[_resident((n, D_MODEL)),
                  _resident((1, D_MODEL)),
                  pl.BlockSpec((D_MODEL, FF_TILE), gcol), pl.BlockSpec((D_MODEL, FF_TILE), vcol),
                  pl.BlockSpec((CONV_W, FF_TILE), gcol), pl.BlockSpec((CONV_W, FF_TILE), vcol),
                  pl.BlockSpec((1, FF_TILE), gcol), pl.BlockSpec((1, FF_TILE), vcol),
                  pl.BlockSpec((FF_TILE, D_MODEL), lambda f: (f, 0)),
                  _resident((1, D_MODEL)),
                  pl.BlockSpec((n, FF_TILE), gcol), pl.BlockSpec((n, FF_TILE), vcol),
                  pl.BlockSpec((n, FF_TILE), gcol), pl.BlockSpec((n, FF_TILE), vcol)],
        out_specs=[pl.BlockSpec((n, D_MODEL), lambda f: (0, 0)),
                   pl.BlockSpec((n, FF_TILE), gcol),
                   pl.BlockSpec((n, FF_TILE), gcol)],
        out_shape=[jax.ShapeDtypeStruct((n, D_MODEL), F32),
                   jax.ShapeDtypeStruct((n, D_FF), F32),
                   jax.ShapeDtypeStruct((n, D_FF), F32)],
        scratch_shapes=[pltpu.VMEM((n, D_MODEL), BF16), pltpu.VMEM((n, D_MODEL), F32)],
        compiler_params=_params(("arbitrary",), 32),
        name="ffn_sample",
    )(x, gn, wu, wu, cw, cw, cb, cb, wd, fg, s0, s0, s1, s1)


def _rope_tables(pos):
    half = HEAD_DIM // 2
    inv = ROPE_THETA ** (-jnp.arange(half, dtype=F32) / half)
    ang = pos.astype(F32)[:, None] * inv[None, :]
    cos, sin = jnp.cos(ang), jnp.sin(ang)
    reps = LANES // HEAD_DIM
    return (jnp.concatenate([cos, cos] * reps, axis=1),
            jnp.concatenate([-sin, sin] * reps, axis=1))


def _ssm_params(a_re, a_im, b_re, b_im, c_re, c_im, d, log_dt):
    dt = jnp.exp(log_dt)[:, None]
    mag = jnp.exp(a_re * dt)
    lr, li = mag * jnp.cos(a_im * dt), mag * jnp.sin(a_im * dt)
    nr, ni = lr - 1.0, li
    den = a_re * a_re + a_im * a_im
    qr = (nr * a_re + ni * a_im) / den
    qi = (ni * a_re - nr * a_im) / den
    bbr = qr[..., None] * b_re - qi[..., None] * b_im
    bbi = qr[..., None] * b_im + qi[..., None] * b_re
    same_group = jnp.eye(GROUP_CHUNK, dtype=bool)[None, :, None, :, None]

    def block_diag(t, rows, cols):
        blk = jnp.where(same_group, t[:, :, :, None, :], 0.0)
        return blk.reshape(N_CHUNK, rows, cols).astype(BF16)

    def in_blocks(bb):
        t = bb.reshape(N_CHUNK, GROUP_CHUNK, SSM_STATE, SSM_GROUP).transpose(0, 1, 3, 2)
        return block_diag(t, CH_U, CH_S)

    def out_blocks(cc):
        t = cc.reshape(N_CHUNK, GROUP_CHUNK, SSM_GROUP, SSM_STATE).transpose(0, 1, 3, 2)
        return block_diag(t, CH_S, CH_U)

    return dict(bre=in_blocks(bbr), bim=in_blocks(bbi), cre=out_blocks(c_re), cimn=out_blocks(-c_im),
                lre=lr.reshape(1, N_STATE), lim=li.reshape(1, N_STATE), d=d.reshape(1, SSM_WIDTH))


def kernel(x_prompt, x_sample, cache_k, cache_v, state_ssm_re, state_ssm_im, state_conv, attn_norm_g, w_in, attn_sinks, ssm_a_re, ssm_a_im, ssm_b_re, ssm_b_im, ssm_c_re, ssm_c_im, ssm_d, ssm_log_dt, w_glu, b_glu, attn_out_norm_g, ssm_out_norm_g, w_out, ffn_norm_g, w_up, conv_w, conv_b, w_down, final_norm_g):
    batch, seq, _ = x_prompt.shape
    nsamp, dec_seq, _ = x_sample.shape
    wbuf = cache_k.shape[2]
    assert dec_seq == 1 and wbuf == WINDOW and seq % SSM_TILE == 0
    assert PAST_LEN >= wbuf

    w_in_b, w_glu_b, w_out_b = w_in.astype(BF16), w_glu.astype(BF16), w_out.astype(BF16)
    w_up_b, w_down_b = w_up.astype(BF16), w_down.astype(BF16)

    cos_p, sin_p = _rope_tables(jnp.arange(seq, dtype=jnp.int32))
    cos_s, sin_s = _rope_tables(jnp.full((nsamp,), PAST_LEN, dtype=jnp.int32))

    row2 = lambda a: a.reshape(1, -1)
    fg = row2(final_norm_g)
    head_is_lo = (jnp.arange(N_HEADS) < KV_REP)[None, :, None]

    xp = x_prompt.reshape(batch * seq, D_MODEL)
    xs = x_sample.reshape(nsamp, D_MODEL)
    tm_p = 512
    outs = {k: [] for k in ("kp", "vp", "hrp", "hip", "cp", "ks", "vs", "hrs", "his", "cs")}

    for l in range(DEPTH):
        last = l == DEPTH - 1
        ssm = _ssm_params(ssm_a_re[l], ssm_a_im[l], ssm_b_re[l], ssm_b_im[l], ssm_c_re[l], ssm_c_im[l],
                          ssm_d[l], ssm_log_dt[l])
        g_in, g_a, g_s, g_f = row2(attn_norm_g[l]), row2(attn_out_norm_g[l]), row2(ssm_out_norm_g[l]), row2(ffn_norm_g[l])
        bg = row2(b_glu[l])
        cb = row2(conv_b[l])

        q, k, v, u = _in_proj(xp, g_in, w_in_b[l], cos_p, sin_p, tm_p)
        a = _attn_prompt(q, k, v, attn_sinks[l], g_a, batch, seq)
        s, hre, him = _ssm_prompt(u, ssm, w_glu_b[l], bg, g_s, batch, seq)
        x1 = _out_proj(xp, a, s, w_out_b[l], tm_p)
        xp, cg, cv = _ffn_prompt(x1, g_f, w_up_b[l], conv_w[l], cb, w_down_b[l], fg, batch, seq, tm_p, last)
        outs["kp"].append(k.reshape(batch, seq, N_KV_HEADS, HEAD_DIM)[:, seq - WINDOW:])
        outs["vp"].append(v.reshape(batch, seq, N_KV_HEADS, HEAD_DIM)[:, seq - WINDOW:])
        outs["hrp"].append(hre.reshape(batch, SSM_GROUPS, SSM_STATE))
        outs["hip"].append(him.reshape(batch, SSM_GROUPS, SSM_STATE))
        unblock = lambda t: t.transpose(0, 2, 1, 3).reshape(batch, CONV_W - 1, D_FF)
        outs["cp"].append(jnp.concatenate([unblock(cg), unblock(cv)], axis=-1))

        q, k, v, u = _in_proj(xs, g_in, w_in_b[l], cos_s, sin_s, nsamp)
        qh = q.reshape(nsamp, N_HEADS, HEAD_DIM)
        zq = jnp.zeros_like(qh)
        qbd = jnp.where(head_is_lo, jnp.concatenate([qh, zq], axis=-1), jnp.concatenate([zq, qh], axis=-1))
        o = _attn_sample(qbd, cache_k[l].reshape(nsamp, wbuf, KV_COLS), cache_v[l].reshape(nsamp, wbuf, KV_COLS),
                         k.reshape(nsamp, 1, KV_COLS), v.reshape(nsamp, 1, KV_COLS),
                         attn_sinks[l].reshape(N_HEADS, 1), 8)
        a_raw = jnp.concatenate([o[:, :KV_REP, :HEAD_DIM].reshape(nsamp, -1),
                                 o[:, KV_REP:, HEAD_DIM:].reshape(nsamp, -1)], axis=1)
        a = _rms_cast(a_raw, g_a)
        s, hre, him = _ssm_sample(u, state_ssm_re[l].reshape(nsamp, N_STATE), state_ssm_im[l].reshape(nsamp, N_STATE),
                                  ssm, w_glu_b[l], bg, g_s)
        x1 = _out_proj(xs, a, s, w_out_b[l], nsamp)
        xs, ug, uv = _ffn_sample(x1, g_f, w_up_b[l], conv_w[l], cb, w_down_b[l], fg,
                                 state_conv[l, :, 0, :], state_conv[l, :, 1, :], last)
        k4 = k.reshape(nsamp, 1, N_KV_HEADS, HEAD_DIM)
        v4 = v.reshape(nsamp, 1, N_KV_HEADS, HEAD_DIM)
        outs["ks"].append(jnp.concatenate([cache_k[l], k4], axis=1)[:, -wbuf:])
        outs["vs"].append(jnp.concatenate([cache_v[l], v4], axis=1)[:, -wbuf:])
        outs["hrs"].append(hre.reshape(nsamp, SSM_GROUPS, SSM_STATE))
        outs["his"].append(him.reshape(nsamp, SSM_GROUPS, SSM_STATE))
        outs["cs"].append(jnp.stack([state_conv[l, :, 1, :], jnp.concatenate([ug, uv], axis=-1)], axis=1))

    st = lambda name: jnp.stack(outs[name], axis=0)
    return (xp.reshape(batch, seq, D_MODEL), xs.reshape(nsamp, 1, D_MODEL),
            st("kp"), st("vp"), st("hrp"), st("hip"), st("cp"),
            st("ks"), st("vs"), st("hrs"), st("his"), st("cs"))
```

```python
import functools

import jax
import jax.numpy as jnp
from jax import lax
from jax.experimental import pallas as pl
from jax.experimental.pallas import tpu as pltpu

F32 = jnp.float32
BF16 = jnp.bfloat16

D_MODEL = 2048
DEPTH = 4
ATTN_WIDTH = 1024
SSM_WIDTH = 1024
HEAD_DIM = 64
N_HEADS = 16
N_KV_HEADS = 2
KV_REP = 8
KV_COLS = N_KV_HEADS * HEAD_DIM
WINDOW = 128
ROPE_THETA = 10000.0
SSM_GROUP = 16
SSM_GROUPS = 64
SSM_STATE = 64
N_STATE = SSM_GROUPS * SSM_STATE
D_FF = 5632
CONV_W = 3
RMS_EPS = 1e-6
IN_COLS = ATTN_WIDTH + 2 * KV_COLS + SSM_WIDTH
NEG_INF = -1e30
PAST_LEN = 16384
SCORE_SCALE = HEAD_DIM ** -0.5
assert SCORE_SCALE == 2.0 ** -3

LANES = 128
SUBLANES = 8
MXU_DIM = 256
MIB = 1024 * 1024
U_BLOCKS = SSM_WIDTH // LANES

GROUP_CHUNK = 16
N_CHUNK = SSM_GROUPS // GROUP_CHUNK
CH_U = GROUP_CHUNK * SSM_GROUP
CH_S = GROUP_CHUNK * SSM_STATE
BLOCKS_PER_CHUNK = CH_U // LANES

SEG_LEN = 32
SSM_TILE = SUBLANES * SEG_LEN
SCAN_LANES = 512


def _rms(x, g):
    ms = jnp.mean(x * x, axis=-1, keepdims=True)
    return x * lax.rsqrt(ms + RMS_EPS) * g


def _params(sem, vmem_mib):
    return pltpu.CompilerParams(dimension_semantics=sem, vmem_limit_bytes=vmem_mib * MIB)


def _resident(shape):
    nd = len(shape)
    return pl.BlockSpec(shape, lambda *_: (0,) * nd, pipeline_mode=pl.Buffered(1))


def _of_layer(l, shape):
    nd = len(shape)
    return pl.BlockSpec((None,) + tuple(shape), lambda *_: (l,) + (0,) * nd, pipeline_mode=pl.Buffered(1))


def _in_proj_kernel(x_ref, g_ref, w_ref, cos_ref, sin_ref, q_ref, k_ref, v_ref, u_ref):
    h = _rms(x_ref[...], g_ref[...]).astype(BF16)
    proj = jnp.dot(h, w_ref[...], preferred_element_type=F32)
    cos = cos_ref[...]
    sin = sin_ref[...]
    lane = lax.broadcasted_iota(jnp.int32, cos.shape, 1)
    first_half = (lane & (HEAD_DIM - 1)) < (HEAD_DIM // 2)

    def rope(blk):
        partner = jnp.where(first_half,
                            pltpu.roll(blk, LANES - HEAD_DIM // 2, 1),
                            pltpu.roll(blk, HEAD_DIM // 2, 1))
        return blk * cos + partner * sin

    for j in range(ATTN_WIDTH // LANES):
        q_ref[:, j * LANES:(j + 1) * LANES] = (rope(proj[:, j * LANES:(j + 1) * LANES]) * SCORE_SCALE).astype(BF16)
    k_ref[...] = rope(proj[:, ATTN_WIDTH:ATTN_WIDTH + KV_COLS])
    v_ref[...] = proj[:, ATTN_WIDTH + KV_COLS:ATTN_WIDTH + 2 * KV_COLS]
    u0 = ATTN_WIDTH + 2 * KV_COLS
    for j in range(U_BLOCKS):
        u_ref[j] = proj[:, u0 + j * LANES:u0 + (j + 1) * LANES]


def _in_proj(l, x, g, w, cos, sin, tm):
    m = x.shape[0]
    pos_tiles = cos.shape[0] // tm
    row = lambda i: (i, 0)
    return pl.pallas_call(
        _in_proj_kernel,
        grid=(m // tm,),
        in_specs=[pl.BlockSpec((tm, D_MODEL), row),
                  _of_layer(l, (1, D_MODEL)),
                  _of_layer(l, (D_MODEL, IN_COLS)),
                  pl.BlockSpec((tm, LANES), lambda i: (i % pos_tiles, 0)),
                  pl.BlockSpec((tm, LANES), lambda i: (i % pos_tiles, 0))],
        out_specs=[pl.BlockSpec((tm, ATTN_WIDTH), row),
                   pl.BlockSpec((tm, KV_COLS), row),
                   pl.BlockSpec((tm, KV_COLS), row),
                   pl.BlockSpec((U_BLOCKS, tm, LANES), lambda i: (0, i, 0))],
        out_shape=[jax.ShapeDtypeStruct((m, ATTN_WIDTH), BF16),
                   jax.ShapeDtypeStruct((m, KV_COLS), F32),
                   jax.ShapeDtypeStruct((m, KV_COLS), F32),
                   jax.ShapeDtypeStruct((U_BLOCKS, m, LANES), F32)],
        compiler_params=_params(("arbitrary",), 48),
        name="in_proj",
    )(x, g, w, cos, sin)


def _attn_prompt_kernel(layer, sinks_ref, q_ref, kc_ref, kp_ref, vc_ref, vp_ref, gn_ref, a_ref):
    j = pl.program_id(1)
    lane = lax.broadcasted_iota(jnp.int32, (WINDOW, LANES), 1)
    lo = lane < HEAD_DIM
    qi = lax.broadcasted_iota(jnp.int32, (WINDOW, 2 * WINDOW), 0)
    kj = lax.broadcasted_iota(jnp.int32, (WINDOW, 2 * WINDOW), 1)
    diff = qi + WINDOW - kj
    first_key = jnp.where(j > 0, 0, WINDOW)
    valid = (diff >= 0) & (diff <= WINDOW) & (kj >= first_key)
    zero = jnp.zeros((WINDOW, LANES), F32)

    def halves(x, g):
        r = pltpu.roll(x, HEAD_DIM, 1)
        if g == 0:
            return jnp.where(lo, x, zero), jnp.where(lo, zero, r)
        return jnp.where(lo, r, zero), jnp.where(lo, zero, x)

    kp, kc, vp, vc = kp_ref[...], kc_ref[...], vp_ref[...], vc_ref[...]
    pairs = KV_REP // 2
    out_blocks = []
    for g in range(N_KV_HEADS):
        kp_lo, kp_hi = halves(kp, g)
        kc_lo, kc_hi = halves(kc, g)
        vp_lo, vp_hi = halves(vp, g)
        vc_lo, vc_hi = halves(vc, g)
        k_lo = jnp.concatenate([kp_lo, kc_lo], axis=0).astype(BF16)
        k_hi = jnp.concatenate([kp_hi, kc_hi], axis=0).astype(BF16)
        v_lo = jnp.concatenate([vp_lo, vc_lo], axis=0).astype(BF16)
        v_hi = jnp.concatenate([vp_hi, vc_hi], axis=0).astype(BF16)
        nt = (((1,), (1,)), ((), ()))

        def softmax(s, head):
            s = jnp.where(valid, s, NEG_INF)
            sink = sinks_ref[layer, head]
            m = jnp.maximum(jnp.max(s, axis=-1, keepdims=True), sink)
            p = jnp.exp(s - m)
            denom = jnp.sum(p, axis=-1, keepdims=True) + jnp.exp(sink - m)
            return (p / denom).astype(BF16)

        qg = jnp.concatenate([q_ref[:, (g * pairs + p) * LANES:(g * pairs + p + 1) * LANES]
                              for p in range(pairs)], axis=0)
        s_even = lax.dot_general(qg, k_lo, nt, preferred_element_type=F32)
        s_odd = lax.dot_general(qg, k_hi, nt, preferred_element_type=F32)
        p_even = jnp.concatenate(
            [softmax(s_even[p * WINDOW:(p + 1) * WINDOW], g * KV_REP + 2 * p) for p in range(pairs)], axis=0)
        p_odd = jnp.concatenate(
            [softmax(s_odd[p * WINDOW:(p + 1) * WINDOW], g * KV_REP + 2 * p + 1) for p in range(pairs)], axis=0)
        o = (jnp.dot(p_even, v_lo, preferred_element_type=F32)
             + jnp.dot(p_odd, v_hi, preferred_element_type=F32))
        out_blocks += [o[p * WINDOW:(p + 1) * WINDOW] for p in range(pairs)]

    ssq = out_blocks[0] * out_blocks[0]
    for blk in out_blocks[1:]:
        ssq = ssq + blk * blk
    inv = lax.rsqrt(jnp.sum(ssq, axis=-1, keepdims=True) / ATTN_WIDTH + RMS_EPS)
    for i, blk in enumerate(out_blocks):
        a_ref[:, i * LANES:(i + 1) * LANES] = (blk * inv * gn_ref[:, i * LANES:(i + 1) * LANES]).astype(BF16)


def _attn_prompt(l, q, k, v, sinks, gn, batch, seq):
    nb = seq // WINDOW
    cur = lambda b, j: (b * nb + j, 0)
    prev = lambda b, j: (b * nb + jnp.maximum(j - 1, 0), 0)
    kv_spec = lambda im: pl.BlockSpec((WINDOW, KV_COLS), im)
    return pl.pallas_call(
        functools.partial(_attn_prompt_kernel, l),
        grid=(batch, nb),
        in_specs=[pl.BlockSpec(memory_space=pltpu.SMEM),
                  pl.BlockSpec((WINDOW, ATTN_WIDTH), cur),
                  kv_spec(cur), kv_spec(prev), kv_spec(cur), kv_spec(prev),
                  _of_layer(l, (1, ATTN_WIDTH))],
        out_specs=pl.BlockSpec((WINDOW, ATTN_WIDTH), cur),
        out_shape=jax.ShapeDtypeStruct((batch * seq, ATTN_WIDTH), BF16),
        compiler_params=_params(("arbitrary", "arbitrary"), 32),
        name="attn_prompt",
    )(sinks, q, k, k, v, v, gn)


def _attn_sample_kernel(qbd_ref, kc_ref, vc_ref, kn_ref, vn_ref, sink_ref, o_ref):
    sink = sink_ref[...]
    nt = (((1,), (1,)), ((), ()))
    for i in range(qbd_ref.shape[0]):
        qb = qbd_ref[i]
        kn, vn = kn_ref[i], vn_ref[i]
        s = lax.dot_general(qb, kc_ref[i].astype(BF16), nt, preferred_element_type=F32)
        s_new = jnp.sum(qb.astype(F32) * kn, axis=-1, keepdims=True)
        m = jnp.maximum(jnp.maximum(jnp.max(s, axis=-1, keepdims=True), s_new), sink)
        p = jnp.exp(s - m)
        p_new = jnp.exp(s_new - m)
        denom = jnp.sum(p, axis=-1, keepdims=True) + p_new + jnp.exp(sink - m)
        o_ref[i] = (jnp.dot((p / denom).astype(BF16), vc_ref[i].astype(BF16), preferred_element_type=F32)
                    + (p_new / denom) * vn)


def _attn_sample(l, qbd, kc, vc, kn, vn, sink_col, bb):
    nbatch = qbd.shape[0]
    blk = lambda *s: pl.BlockSpec((bb,) + s, lambda i: (i, 0, 0))
    cache = pl.BlockSpec((None, bb, WINDOW, KV_COLS), lambda i: (l, i, 0, 0))
    return pl.pallas_call(
        _attn_sample_kernel,
        grid=(nbatch // bb,),
        in_specs=[blk(N_HEADS, LANES), cache, cache,
                  blk(1, KV_COLS), blk(1, KV_COLS), _of_layer(l, (N_HEADS, 1))],
        out_specs=blk(N_HEADS, LANES),
        out_shape=jax.ShapeDtypeStruct((nbatch, N_HEADS, LANES), F32),
        compiler_params=_params(("arbitrary",), 32),
        name="attn_sample",
    )(qbd, kc, vc, kn, vn, sink_col)


def _rms_cast_kernel(x_ref, g_ref, o_ref):
    o_ref[...] = _rms(x_ref[...], g_ref[...]).astype(BF16)


def _rms_cast(l, x, g):
    n, d = x.shape
    return pl.pallas_call(
        _rms_cast_kernel,
        grid=(1,),
        in_specs=[_resident((n, d)), _of_layer(l, (1, d))],
        out_specs=pl.BlockSpec((n, d), lambda i: (0, 0)),
        out_shape=jax.ShapeDtypeStruct(x.shape, BF16),
        name="rms_cast",
    )(x, g)


def _glu_norm(z, wg_ref, bg_ref, gn_ref):
    gate = jax.nn.sigmoid(jnp.dot(z.astype(BF16), wg_ref[...], preferred_element_type=F32) + bg_ref[...])
    return _rms(z * gate, gn_ref[...])


def _ssm_operands(l, ssm, wg, bg, gn):
    arrays = (ssm["bre"], ssm["bim"], ssm["cre"], ssm["cimn"], ssm["lre"], ssm["lim"], ssm["d"], wg, bg, gn)
    return arrays, [_of_layer(l, a.shape[1:]) for a in arrays]


def _ssm_prompt_kernel(u_ref, perm_ref, bre_ref, bim_ref, cre_ref, cimn_ref, lre_ref, lim_ref, d_ref,
                       wg_ref, bg_ref, gn_ref,
                       s_ref, hre_ref, him_ref,
                       xre_s, xim_s, hbre_s, hbim_s, pre_s, pim_s, cre_s, cim_s, hinre_s, hinim_s, y_s):
    b = pl.program_id(0)
    j = pl.program_id(1)

    @pl.when((b == 0) & (j == 0))
    def _():
        lr, li = lre_ref[...], lim_ref[...]
        pr, pi = lr, li
        for i in range(SEG_LEN):
            if i:
                pr, pi = pr * lr - pi * li, pr * li + pi * lr
            pre_s[i * SUBLANES:(i + 1) * SUBLANES, :] = jnp.broadcast_to(pr, (SUBLANES, N_STATE))
            pim_s[i * SUBLANES:(i + 1) * SUBLANES, :] = jnp.broadcast_to(pi, (SUBLANES, N_STATE))

    @pl.when(j == 0)
    def _():
        cre_s[...] = jnp.zeros_like(cre_s)
        cim_s[...] = jnp.zeros_like(cim_s)

    u_nat = jnp.concatenate([u_ref[jb] for jb in range(U_BLOCKS)], axis=1)
    u_perm = jnp.dot(perm_ref[...], u_nat.astype(BF16), preferred_element_type=F32).astype(BF16)

    last = (SEG_LEN - 1) * SUBLANES
    for c in range(N_CHUNK):
        uc = u_perm[:, c * CH_U:(c + 1) * CH_U]
        chunk = slice(c * CH_S, (c + 1) * CH_S)
        xre_s[:, chunk] = jnp.dot(uc, bre_ref[c], preferred_element_type=F32)
        xim_s[:, chunk] = jnp.dot(uc, bim_ref[c], preferred_element_type=F32)

        for lc in range(CH_S // SCAN_LANES):
            sl = slice(c * CH_S + lc * SCAN_LANES, c * CH_S + (lc + 1) * SCAN_LANES)
            lr = jnp.broadcast_to(lre_ref[:, sl], (SUBLANES, SCAN_LANES))
            li = jnp.broadcast_to(lim_ref[:, sl], (SUBLANES, SCAN_LANES))
            hr = xre_s[0:SUBLANES, sl]
            hi = xim_s[0:SUBLANES, sl]
            for i in range(1, SEG_LEN):
                rows = slice(i * SUBLANES, (i + 1) * SUBLANES)
                hr, hi = (lr * hr - li * hi) + xre_s[rows, sl], (lr * hi + li * hr) + xim_s[rows, sl]
                xre_s[rows, sl] = hr
                xim_s[rows, sl] = hi

        l32r = pre_s[last:last + 1, chunk]
        l32i = pim_s[last:last + 1, chunk]
        cr, ci = cre_s[:, chunk], cim_s[:, chunk]
        for seg in range(SUBLANES):
            hinre_s[seg:seg + 1, chunk] = cr
            hinim_s[seg:seg + 1, chunk] = ci
            er = xre_s[last + seg:last + seg + 1, chunk]
            ei = xim_s[last + seg:last + seg + 1, chunk]
            cr, ci = (l32r * cr - l32i * ci) + er, (l32r * ci + l32i * cr) + ei
        cre_s[:, chunk] = cr
        cim_s[:, chunk] = ci

        for lc in range(CH_S // SCAN_LANES):
            sl = slice(c * CH_S + lc * SCAN_LANES, c * CH_S + (lc + 1) * SCAN_LANES)
            hr_in = jnp.concatenate([hinre_s[:, sl]] * 2, axis=0)
            hi_in = jnp.concatenate([hinim_s[:, sl]] * 2, axis=0)
            for i2 in range(SEG_LEN // 2):
                rows = slice(i2 * 2 * SUBLANES, (i2 + 1) * 2 * SUBLANES)
                pr, pi = pre_s[rows, sl], pim_s[rows, sl]
                hbre_s[rows, sl] = (xre_s[rows, sl] + (pr * hr_in - pi * hi_in)).astype(BF16)
                hbim_s[rows, sl] = (xim_s[rows, sl] + (pr * hi_in + pi * hr_in)).astype(BF16)

        y = (jnp.dot(hbre_s[:, chunk], cre_ref[c], preferred_element_type=F32)
             + jnp.dot(hbim_s[:, chunk], cimn_ref[c], preferred_element_type=F32))
        for h in range(BLOCKS_PER_CHUNK):
            y_s[c * BLOCKS_PER_CHUNK + h] = y[:, h * LANES:(h + 1) * LANES]

    y_nat = jnp.concatenate(
        [jnp.concatenate([y_s[jb, pl.ds(i0 * SUBLANES + seg, SUBLANES, stride=SUBLANES), :]
                          for seg in range(SUBLANES) for i0 in range(0, SEG_LEN, SUBLANES)], axis=0)
         for jb in range(U_BLOCKS)], axis=1)
    z = jax.nn.gelu(y_nat + d_ref[...] * u_nat)
    s_ref[...] = _glu_norm(z, wg_ref, bg_ref, gn_ref).astype(BF16)

    @pl.when(j == pl.num_programs(1) - 1)
    def _():
        hre_ref[...] = cre_s[...]
        him_ref[...] = cim_s[...]


def _ssm_prompt(l, u, perm, ssm, wg, bg, gn, batch, seq):
    nt = seq // SSM_TILE
    row = lambda b, j: (b * nt + j, 0)
    st = lambda b, j: (b, 0, 0)
    arrays, specs = _ssm_operands(l, ssm, wg, bg, gn)
    scratch = [
        pltpu.VMEM((SSM_TILE, N_STATE), F32),
        pltpu.VMEM((SSM_TILE, N_STATE), F32),
        pltpu.VMEM((SSM_TILE, N_STATE), BF16),
        pltpu.VMEM((SSM_TILE, N_STATE), BF16),
        pltpu.VMEM((SSM_TILE, N_STATE), F32),
        pltpu.VMEM((SSM_TILE, N_STATE), F32),
        pltpu.VMEM((1, N_STATE), F32),
        pltpu.VMEM((1, N_STATE), F32),
        pltpu.VMEM((SUBLANES, N_STATE), F32),
        pltpu.VMEM((SUBLANES, N_STATE), F32),
        pltpu.VMEM((U_BLOCKS, SSM_TILE, LANES), F32),
    ]
    return pl.pallas_call(
        _ssm_prompt_kernel,
        grid=(batch, nt),
        in_specs=[pl.BlockSpec((U_BLOCKS, SSM_TILE, LANES), lambda b, j: (0, b * nt + j, 0)),
                  _resident((SSM_TILE, SSM_TILE))] + specs,
        out_specs=[pl.BlockSpec((SSM_TILE, SSM_WIDTH), row),
                   pl.BlockSpec((None, 1, N_STATE), st),
                   pl.BlockSpec((None, 1, N_STATE), st)],
        out_shape=[jax.ShapeDtypeStruct((batch * seq, SSM_WIDTH), BF16),
                   jax.ShapeDtypeStruct((batch, 1, N_STATE), F32),
                   jax.ShapeDtypeStruct((batch, 1, N_STATE), F32)],
        scratch_shapes=scratch,
        compiler_params=_params(("arbitrary", "arbitrary"), 52),
        name="ssm_prompt",
    )(u, perm, *arrays)


def _ssm_sample_kernel(u_ref, h0re_ref, h0im_ref, bre_ref, bim_ref, cre_ref, cimn_ref, lre_ref, lim_ref, d_ref,
                       wg_ref, bg_ref, gn_ref, s_ref, hre_ref, him_ref):
    u = jnp.concatenate([u_ref[jb] for jb in range(U_BLOCKS)], axis=1)
    zs = []
    for c in range(N_CHUNK):
        cs = slice(c * CH_U, (c + 1) * CH_U)
        ss = slice(c * CH_S, (c + 1) * CH_S)
        uc = u[:, cs].astype(BF16)
        lr, li = lre_ref[:, ss], lim_ref[:, ss]
        h0r, h0i = h0re_ref[:, ss], h0im_ref[:, ss]
        hr = jnp.dot(uc, bre_ref[c], preferred_element_type=F32) + (lr * h0r - li * h0i)
        hi = jnp.dot(uc, bim_ref[c], preferred_element_type=F32) + (lr * h0i + li * h0r)
        hre_ref[:, ss] = hr
        him_ref[:, ss] = hi
        y = (jnp.dot(hr.astype(BF16), cre_ref[c], preferred_element_type=F32)
             + jnp.dot(hi.astype(BF16), cimn_ref[c], preferred_element_type=F32))
        zs.append(jax.nn.gelu(y + d_ref[:, cs] * u[:, cs]))
    z = jnp.concatenate(zs, axis=1)
    s_ref[...] = _glu_norm(z, wg_ref, bg_ref, gn_ref).astype(BF16)


def _ssm_sample(l, u, h0re, h0im, ssm, wg, bg, gn):
    n = u.shape[1]
    arrays, specs = _ssm_operands(l, ssm, wg, bg, gn)
    whole = lambda *s: pl.BlockSpec(s, lambda i: (0,) * len(s))
    return pl.pallas_call(
        _ssm_sample_kernel,
        grid=(1,),
        in_specs=[_resident((U_BLOCKS, n, LANES)), _of_layer(l, (n, N_STATE)), _of_layer(l, (n, N_STATE))] + specs,
        out_specs=[whole(n, SSM_WIDTH), whole(n, N_STATE), whole(n, N_STATE)],
        out_shape=[jax.ShapeDtypeStruct((n, SSM_WIDTH), BF16),
                   jax.ShapeDtypeStruct((n, N_STATE), F32),
                   jax.ShapeDtypeStruct((n, N_STATE), F32)],
        compiler_params=_params(("arbitrary",), 40),
        name="ssm_sample",
    )(u, h0re, h0im, *arrays)


def _out_proj_kernel(x_ref, a_ref, s_ref, w_ref, o_ref):
    acc = (jnp.dot(a_ref[...], w_ref[0:ATTN_WIDTH, :], preferred_element_type=F32)
           + jnp.dot(s_ref[...], w_ref[ATTN_WIDTH:, :], preferred_element_type=F32))
    o_ref[...] = x_ref[...] + acc


def _out_proj(l, x, a, s, w, tm):
    m = x.shape[0]
    row = lambda i: (i, 0)
    return pl.pallas_call(
        _out_proj_kernel,
        grid=(m // tm,),
        in_specs=[pl.BlockSpec((tm, D_MODEL), row),
                  pl.BlockSpec((tm, ATTN_WIDTH), row),
                  pl.BlockSpec((tm, SSM_WIDTH), row),
                  _of_layer(l, (D_MODEL, D_MODEL))],
        out_specs=pl.BlockSpec((tm, D_MODEL), row),
        out_shape=jax.ShapeDtypeStruct((m, D_MODEL), F32),
        compiler_params=_params(("arbitrary",), 48),
        name="out_proj",
    )(x, a, s, w)


FF_TILE = 512
N_FF = D_FF // FF_TILE
FF_SUB = MXU_DIM
CONV_PAD = SUBLANES


def _ffn_weight_specs(l, idx):
    gcol = lambda *g: (l, 0, idx(*g))
    vcol = lambda *g: (l, 0, N_FF + idx(*g))
    return [pl.BlockSpec((None, D_MODEL, FF_TILE), gcol), pl.BlockSpec((None, D_MODEL, FF_TILE), vcol),
            pl.BlockSpec((None, CONV_W, FF_TILE), gcol), pl.BlockSpec((None, CONV_W, FF_TILE), vcol),
            pl.BlockSpec((None, 1, FF_TILE), gcol), pl.BlockSpec((None, 1, FF_TILE), vcol),
            pl.BlockSpec((None, FF_TILE, D_MODEL), lambda *g: (l, idx(*g), 0))]


def _ffn_prompt_kernel(final_norm, tiles_per_seq,
                       x_ref, gn_ref, fg_ref, wug_ref, wuv_ref, cwg_ref, cwv_ref, cbg_ref, cbv_ref, wd_ref,
                       o_ref, cg_ref, cv_ref,
                       h_s, extg_s, extv_s, carryg_s, carryv_s):
    m = pl.program_id(0)
    f = pl.program_id(1)
    tm = x_ref.shape[0]

    @pl.when(f == 0)
    def _():
        x = x_ref[...]
        h_s[...] = _rms(x, gn_ref[...]).astype(BF16)
        o_ref[...] = x

    seq_start = (m % tiles_per_seq) == 0

    @pl.when(seq_start)
    def _():
        extg_s[0:CONV_PAD, :] = jnp.zeros((CONV_PAD, FF_TILE), F32)
        extv_s[0:CONV_PAD, :] = jnp.zeros((CONV_PAD, FF_TILE), F32)

    @pl.when(jnp.logical_not(seq_start))
    def _():
        extg_s[0:CONV_PAD, :] = carryg_s[f]
        extv_s[0:CONV_PAD, :] = carryv_s[f]

    subs = [slice(s * FF_SUB, (s + 1) * FF_SUB) for s in range(FF_TILE // FF_SUB)]
    for cs in subs:
        extg_s[CONV_PAD:, cs] = jnp.dot(h_s[...], wug_ref[:, cs], preferred_element_type=F32)
        extv_s[CONV_PAD:, cs] = jnp.dot(h_s[...], wuv_ref[:, cs], preferred_element_type=F32)

    def conv(ext_s, cw_ref, cb_ref, cs):
        out = cb_ref[:, cs]
        for t in range(CONV_W):
            lag = CONV_W - 1 - t
            out = out + cw_ref[t:t + 1, cs] * ext_s[CONV_PAD - lag:CONV_PAD - lag + tm, cs]
        return out

    down = None
    for cs in subs:
        act = (jax.nn.silu(conv(extg_s, cwg_ref, cbg_ref, cs)) * conv(extv_s, cwv_ref, cbv_ref, cs)).astype(BF16)
        part = jnp.dot(act, wd_ref[cs, :], preferred_element_type=F32)
        down = part if down is None else down + part
    o_ref[...] += down

    carryg_s[f] = extg_s[tm:tm + CONV_PAD, :]
    carryv_s[f] = extv_s[tm:tm + CONV_PAD, :]
    cg_ref[f] = extg_s[tm + CONV_PAD - (CONV_W - 1):tm + CONV_PAD, :]
    cv_ref[f] = extv_s[tm + CONV_PAD - (CONV_W - 1):tm + CONV_PAD, :]

    if final_norm:
        @pl.when(f == pl.num_programs(1) - 1)
        def _():
            o_ref[...] = _rms(o_ref[...], fg_ref[...])


def _ffn_prompt(l, x, gn, wu, cw, cb, wd, fg, batch, seq, tm, final_norm):
    m = batch * seq
    tiles_per_seq = seq // tm
    row = lambda i, f: (i, 0)
    state = lambda i, f: (i // tiles_per_seq, 0, 0, 0)
    return pl.pallas_call(
        functools.partial(_ffn_prompt_kernel, final_norm, tiles_per_seq),
        grid=(m // tm, N_FF),
        in_specs=[pl.BlockSpec((tm, D_MODEL), row, pipeline_mode=pl.Buffered(1)),
                  _of_layer(l, (1, D_MODEL)), _resident((1, D_MODEL))]
                 + _ffn_weight_specs(l, lambda i, f: f),
        out_specs=[pl.BlockSpec((tm, D_MODEL), row),
                   pl.BlockSpec((None, N_FF, CONV_W - 1, FF_TILE), state),
                   pl.BlockSpec((None, N_FF, CONV_W - 1, FF_TILE), state)],
        out_shape=[jax.ShapeDtypeStruct((m, D_MODEL), F32),
                   jax.ShapeDtypeStruct((batch, N_FF, CONV_W - 1, FF_TILE), F32),
                   jax.ShapeDtypeStruct((batch, N_FF, CONV_W - 1, FF_TILE), F32)],
        scratch_shapes=[pltpu.VMEM((tm, D_MODEL), BF16),
                        pltpu.VMEM((tm + CONV_PAD, FF_TILE), F32),
                        pltpu.VMEM((tm + CONV_PAD, FF_TILE), F32),
                        pltpu.VMEM((N_FF, CONV_PAD, FF_TILE), F32),
                        pltpu.VMEM((N_FF, CONV_PAD, FF_TILE), F32)],
        compiler_params=_params(("arbitrary", "arbitrary"), 56),
        name="ffn_prompt",
    )(x, gn, fg, wu, wu, cw, cw, cb, cb, wd)


def _ffn_sample_kernel(final_norm,
                       x_ref, gn_ref, fg_ref, wug_ref, wuv_ref, cwg_ref, cwv_ref, cbg_ref, cbv_ref, wd_ref,
                       s0g_ref, s0v_ref, s1g_ref, s1v_ref,
                       o_ref, ug_ref, uv_ref, h_s):
    f = pl.program_id(0)

    @pl.when(f == 0)
    def _():
        x = x_ref[...]
        h_s[...] = _rms(x, gn_ref[...]).astype(BF16)
        o_ref[...] = x

    def conv(w_ref, cw_ref, cb_ref, s0_ref, s1_ref, up_ref):
        up = jnp.dot(h_s[...], w_ref[...], preferred_element_type=F32)
        up_ref[...] = up
        return ((cb_ref[...] + cw_ref[0:1, :] * s0_ref[...]) + cw_ref[1:2, :] * s1_ref[...]) + cw_ref[2:3, :] * up

    gate = conv(wug_ref, cwg_ref, cbg_ref, s0g_ref, s1g_ref, ug_ref)
    val = conv(wuv_ref, cwv_ref, cbv_ref, s0v_ref, s1v_ref, uv_ref)
    act = (jax.nn.silu(gate) * val).astype(BF16)
    o_ref[...] += jnp.dot(act, wd_ref[...], preferred_element_type=F32)

    if final_norm:
        @pl.when(f == pl.num_programs(0) - 1)
        def _():
            o_ref[...] = _rms(o_ref[...], fg_ref[...])


def _ffn_sample(l, x, gn, wu, cw, cb, wd, fg, s0, s1, final_norm):
    n = x.shape[0]
    gcol = lambda f: (l, 0, f)
    vcol = lambda f: (l, 0, N_FF + f)
    st = lambda im: pl.BlockSpec((None, n, FF_TILE), im)
    return pl.pallas_call(
        functools.partial(_ffn_sample_kernel, final_norm),
        grid=(N_FF,),
        in_specs=[_resident((n, D_MODEL)), _of_layer(l, (1, D_MODEL)), _resident((1, D_MODEL))]
                 + _ffn_weight_specs(l, lambda f: f)
                 + [st(gcol), st(vcol), st(gcol), st(vcol)],
        out_specs=[pl.BlockSpec((n, D_MODEL), lambda f: (0, 0)),
                   pl.BlockSpec((n, FF_TILE), lambda f: (0, f)),
                   pl.BlockSpec((n, FF_TILE), lambda f: (0, f))],
        out_shape=[jax.ShapeDtypeStruct((n, D_MODEL), F32),
                   jax.ShapeDtypeStruct((n, D_FF), F32),
                   jax.ShapeDtypeStruct((n, D_FF), F32)],
        scratch_shapes=[pltpu.VMEM((n, D_MODEL), BF16)],
        compiler_params=_params(("arbitrary",), 32),
        name="ffn_sample",
    )(x, gn, fg, wu, wu, cw, cw, cb, cb, wd, s0, s0, s1, s1)


def _rope_tables(pos):
    half = HEAD_DIM // 2
    inv = ROPE_THETA ** (-jnp.arange(half, dtype=F32) / half)
    ang = pos.astype(F32)[:, None] * inv[None, :]
    cos, sin = jnp.cos(ang), jnp.sin(ang)
    reps = LANES // HEAD_DIM
    return (jnp.concatenate([cos, cos] * reps, axis=1),
            jnp.concatenate([-sin, sin] * reps, axis=1))


def _ssm_params(a_re, a_im, b_re, b_im, c_re, c_im, d, log_dt):
    depth = a_re.shape[0]
    dt = jnp.exp(log_dt)[..., None]
    mag = jnp.exp(a_re * dt)
    lr, li = mag * jnp.cos(a_im * dt), mag * jnp.sin(a_im * dt)
    nr, ni = lr - 1.0, li
    den = a_re * a_re + a_im * a_im
    qr = (nr * a_re + ni * a_im) / den
    qi = (ni * a_re - nr * a_im) / den
    bbr = qr[..., None] * b_re - qi[..., None] * b_im
    bbi = qr[..., None] * b_im + qi[..., None] * b_re
    same_group = jnp.eye(GROUP_CHUNK, dtype=bool)[None, None, :, None, :, None]

    def block_diag(t, rows, cols):
        t = t.reshape(depth, N_CHUNK, GROUP_CHUNK, t.shape[2], t.shape[3])
        blk = jnp.where(same_group, t[:, :, :, :, None, :], 0.0)
        return blk.reshape(depth, N_CHUNK, rows, cols).astype(BF16)

    return dict(bre=block_diag(bbr.transpose(0, 1, 3, 2), CH_U, CH_S),
                bim=block_diag(bbi.transpose(0, 1, 3, 2), CH_U, CH_S),
                cre=block_diag(c_re.transpose(0, 1, 3, 2), CH_S, CH_U),
                cimn=block_diag((-c_im).transpose(0, 1, 3, 2), CH_S, CH_U),
                lre=lr.reshape(depth, 1, N_STATE), lim=li.reshape(depth, 1, N_STATE),
                d=d.reshape(depth, 1, SSM_WIDTH))


def _segment_permutation():
    r = jnp.arange(SSM_TILE)
    src = (r % SUBLANES) * SEG_LEN + r // SUBLANES
    return (src[:, None] == jnp.arange(SSM_TILE)[None, :]).astype(BF16)


def kernel(x_prompt, x_sample, cache_k, cache_v, state_ssm_re, state_ssm_im, state_conv, attn_norm_g, w_in, attn_sinks, ssm_a_re, ssm_a_im, ssm_b_re, ssm_b_im, ssm_c_re, ssm_c_im, ssm_d, ssm_log_dt, w_glu, b_glu, attn_out_norm_g, ssm_out_norm_g, w_out, ffn_norm_g, w_up, conv_w, conv_b, w_down, final_norm_g):
    batch, seq, _ = x_prompt.shape
    nsamp, dec_seq, _ = x_sample.shape
    wbuf = cache_k.shape[2]
    assert dec_seq == 1 and wbuf == WINDOW and seq % SSM_TILE == 0
    assert PAST_LEN >= wbuf

    w_in_b, w_glu_b, w_out_b = w_in.astype(BF16), w_glu.astype(BF16), w_out.astype(BF16)
    w_up_b, w_down_b = w_up.astype(BF16), w_down.astype(BF16)
    ssm = _ssm_params(ssm_a_re, ssm_a_im, ssm_b_re, ssm_b_im, ssm_c_re, ssm_c_im, ssm_d, ssm_log_dt)
    perm = _segment_permutation()

    cos_p, sin_p = _rope_tables(jnp.arange(seq, dtype=jnp.int32))
    cos_s, sin_s = _rope_tables(jnp.full((nsamp,), PAST_LEN, dtype=jnp.int32))

    rows = lambda a: a.reshape(a.shape[0], 1, a.shape[-1])
    g_in, g_a, g_s, g_f = rows(attn_norm_g), rows(attn_out_norm_g), rows(ssm_out_norm_g), rows(ffn_norm_g)
    bg, cb = rows(b_glu), rows(conv_b)
    fg = final_norm_g.reshape(1, D_MODEL)
    sink_col = attn_sinks.reshape(DEPTH, N_HEADS, 1)
    head_is_lo = (jnp.arange(N_HEADS) < KV_REP)[None, :, None]
    kc = cache_k.reshape(DEPTH, nsamp, wbuf, KV_COLS)
    vc = cache_v.reshape(DEPTH, nsamp, wbuf, KV_COLS)
    h0re = state_ssm_re.reshape(DEPTH, nsamp, N_STATE)
    h0im = state_ssm_im.reshape(DEPTH, nsamp, N_STATE)
    conv0, conv1 = state_conv[:, :, 0, :], state_conv[:, :, 1, :]

    xp = x_prompt.reshape(batch * seq, D_MODEL)
    xs = x_sample.reshape(nsamp, D_MODEL)
    tm_p = 512
    tm_ffn = 1024
    outs = {k: [] for k in ("kp", "vp", "hrp", "hip", "cgp", "cvp", "ks", "vs", "hrs", "his", "ugs", "uvs")}

    for l in range(DEPTH):
        last = l == DEPTH - 1

        q, k, v, u = _in_proj(l, xp, g_in, w_in_b, cos_p, sin_p, tm_p)
        a = _attn_prompt(l, q, k, v, attn_sinks, g_a, batch, seq)
        s, hre, him = _ssm_prompt(l, u, perm, ssm, w_glu_b, bg, g_s, batch, seq)
        x1 = _out_proj(l, xp, a, s, w_out_b, tm_p)
        xp, cg, cv = _ffn_prompt(l, x1, g_f, w_up_b, conv_w, cb, w_down_b, fg, batch, seq, tm_ffn, last)
        outs["kp"].append(k.reshape(batch, seq, KV_COLS)[:, seq - WINDOW:])
        outs["vp"].append(v.reshape(batch, seq, KV_COLS)[:, seq - WINDOW:])
        outs["hrp"].append(hre)
        outs["hip"].append(him)
        outs["cgp"].append(cg)
        outs["cvp"].append(cv)

        q, k, v, u = _in_proj(l, xs, g_in, w_in_b, cos_s, sin_s, nsamp)
        qh = q.reshape(nsamp, N_HEADS, HEAD_DIM)
        zq = jnp.zeros_like(qh)
        qbd = jnp.where(head_is_lo, jnp.concatenate([qh, zq], axis=-1), jnp.concatenate([zq, qh], axis=-1))
        o = _attn_sample(l, qbd, kc, vc, k.reshape(nsamp, 1, KV_COLS), v.reshape(nsamp, 1, KV_COLS), sink_col, 8)
        a_raw = jnp.concatenate([o[:, :KV_REP, :HEAD_DIM].reshape(nsamp, -1),
                                 o[:, KV_REP:, HEAD_DIM:].reshape(nsamp, -1)], axis=1)
        a = _rms_cast(l, a_raw, g_a)
        s, hre, him = _ssm_sample(l, u, h0re, h0im, ssm, w_glu_b, bg, g_s)
        x1 = _out_proj(l, xs, a, s, w_out_b, nsamp)
        xs, ug, uv = _ffn_sample(l, x1, g_f, w_up_b, conv_w, cb, w_down_b, fg, conv0, conv1, last)
        outs["ks"].append(k)
        outs["vs"].append(v)
        outs["hrs"].append(hre)
        outs["his"].append(him)
        outs["ugs"].append(ug)
        outs["uvs"].append(uv)

    st = lambda name: jnp.stack(outs[name], axis=0)
    heads = lambda t: t.reshape(t.shape[:-1] + (N_KV_HEADS, HEAD_DIM))
    states = lambda t: t.reshape(DEPTH, -1, SSM_GROUPS, SSM_STATE)
    unblock = lambda t: t.transpose(0, 1, 3, 2, 4).reshape(DEPTH, batch, CONV_W - 1, D_FF)
    conv_prompt = jnp.concatenate([unblock(st("cgp")), unblock(st("cvp"))], axis=-1)
    k_sample = jnp.concatenate([cache_k[:, :, 1:], heads(st("ks"))[:, :, None]], axis=2)
    v_sample = jnp.concatenate([cache_v[:, :, 1:], heads(st("vs"))[:, :, None]], axis=2)
    conv_sample = jnp.stack([conv1, jnp.concatenate([st("ugs"), st("uvs")], axis=-1)], axis=2)
    return (xp.reshape(batch, seq, D_MODEL), xs.reshape(nsamp, 1, D_MODEL),
            heads(st("kp")), heads(st("vp")), states(st("hrp")), states(st("hip")), conv_prompt,
            k_sample, v_sample, states(st("hrs")), states(st("his")), conv_sample)
```

```python
import functools

import jax
import jax.numpy as jnp
from jax import lax
from jax.experimental import pallas as pl
from jax.experimental.pallas import tpu as pltpu

F32 = jnp.float32
BF16 = jnp.bfloat16

D_MODEL = 2048
DEPTH = 4
ATTN_WIDTH = 1024
SSM_WIDTH = 1024
HEAD_DIM = 64
N_HEADS = 16
N_KV_HEADS = 2
KV_REP = 8
KV_COLS = N_KV_HEADS * HEAD_DIM
WINDOW = 128
ROPE_THETA = 10000.0
SSM_GROUP = 16
SSM_GROUPS = 64
SSM_STATE = 64
N_STATE = SSM_GROUPS * SSM_STATE
D_FF = 5632
CONV_W = 3
RMS_EPS = 1e-6
IN_COLS = ATTN_WIDTH + 2 * KV_COLS + SSM_WIDTH
NEG_INF = -1e30
PAST_LEN = 16384
SCORE_SCALE = HEAD_DIM ** -0.5
assert SCORE_SCALE == 2.0 ** -3

LANES = 128
SUBLANES = 8
MXU_DIM = 256
MIB = 1024 * 1024
U_BLOCKS = SSM_WIDTH // LANES

GROUP_CHUNK = 16
N_CHUNK = SSM_GROUPS // GROUP_CHUNK
CH_U = GROUP_CHUNK * SSM_GROUP
CH_S = GROUP_CHUNK * SSM_STATE
BLOCKS_PER_CHUNK = CH_U // LANES

SEG_LEN = 32
SSM_TILE = SUBLANES * SEG_LEN
SCAN_LANES = 512


def _rms(x, g):
    ms = jnp.mean(x * x, axis=-1, keepdims=True)
    return x * lax.rsqrt(ms + RMS_EPS) * g


def _params(sem, vmem_mib):
    return pltpu.CompilerParams(dimension_semantics=sem, vmem_limit_bytes=vmem_mib * MIB)


def _resident(shape):
    nd = len(shape)
    return pl.BlockSpec(shape, lambda *_: (0,) * nd, pipeline_mode=pl.Buffered(1))


def _of_layer(l, shape):
    nd = len(shape)
    return pl.BlockSpec((None,) + tuple(shape), lambda *_: (l,) + (0,) * nd, pipeline_mode=pl.Buffered(1))


def _in_proj_kernel(x_ref, g_ref, w_ref, cos_ref, sin_ref, q_ref, k_ref, v_ref, u_ref):
    h = _rms(x_ref[...], g_ref[...]).astype(BF16)
    proj = jnp.dot(h, w_ref[...], preferred_element_type=F32)
    cos = cos_ref[...]
    sin = sin_ref[...]
    lane = lax.broadcasted_iota(jnp.int32, cos.shape, 1)
    first_half = (lane & (HEAD_DIM - 1)) < (HEAD_DIM // 2)

    def rope(blk):
        partner = jnp.where(first_half,
                            pltpu.roll(blk, LANES - HEAD_DIM // 2, 1),
                            pltpu.roll(blk, HEAD_DIM // 2, 1))
        return blk * cos + partner * sin

    for j in range(ATTN_WIDTH // LANES):
        q_ref[:, j * LANES:(j + 1) * LANES] = (rope(proj[:, j * LANES:(j + 1) * LANES]) * SCORE_SCALE).astype(BF16)
    k_ref[...] = rope(proj[:, ATTN_WIDTH:ATTN_WIDTH + KV_COLS])
    v_ref[...] = proj[:, ATTN_WIDTH + KV_COLS:ATTN_WIDTH + 2 * KV_COLS]
    u0 = ATTN_WIDTH + 2 * KV_COLS
    for j in range(U_BLOCKS):
        u_ref[j] = proj[:, u0 + j * LANES:u0 + (j + 1) * LANES]


def _in_proj(l, x, g, w, cos, sin, tm):
    m = x.shape[0]
    pos_tiles = cos.shape[0] // tm
    row = lambda i: (i, 0)
    return pl.pallas_call(
        _in_proj_kernel,
        grid=(m // tm,),
        in_specs=[pl.BlockSpec((tm, D_MODEL), row),
                  _of_layer(l, (1, D_MODEL)),
                  _of_layer(l, (D_MODEL, IN_COLS)),
                  pl.BlockSpec((tm, LANES), lambda i: (i % pos_tiles, 0)),
                  pl.BlockSpec((tm, LANES), lambda i: (i % pos_tiles, 0))],
        out_specs=[pl.BlockSpec((tm, ATTN_WIDTH), row),
                   pl.BlockSpec((tm, KV_COLS), row),
                   pl.BlockSpec((tm, KV_COLS), row),
                   pl.BlockSpec((U_BLOCKS, tm, LANES), lambda i: (0, i, 0))],
        out_shape=[jax.ShapeDtypeStruct((m, ATTN_WIDTH), BF16),
                   jax.ShapeDtypeStruct((m, KV_COLS), F32),
                   jax.ShapeDtypeStruct((m, KV_COLS), F32),
                   jax.ShapeDtypeStruct((U_BLOCKS, m, LANES), F32)],
        compiler_params=_params(("arbitrary",), 48),
        name="in_proj",
    )(x, g, w, cos, sin)


ATTN_SUB = 4


def _attn_prompt_kernel(layer, sinks_ref, q_ref, kc_ref, kp_ref, vc_ref, vp_ref, gn_ref, a_ref):
    j = pl.program_id(1)
    lane = lax.broadcasted_iota(jnp.int32, (WINDOW, LANES), 1)
    lo = lane < HEAD_DIM
    qi = lax.broadcasted_iota(jnp.int32, (WINDOW, 2 * WINDOW), 0)
    kj = lax.broadcasted_iota(jnp.int32, (WINDOW, 2 * WINDOW), 1)
    diff = qi + WINDOW - kj
    band = (diff >= 0) & (diff <= WINDOW)
    first_key = jnp.where(j > 0, 0, WINDOW)
    valid = [band & (kj >= first_key)] + [band] * (ATTN_SUB - 1)
    zero = jnp.zeros((WINDOW, LANES), F32)

    def halves(x, g):
        r = pltpu.roll(x, HEAD_DIM, 1)
        if g == 0:
            return jnp.where(lo, x, zero).astype(BF16), jnp.where(lo, zero, r).astype(BF16)
        return jnp.where(lo, r, zero).astype(BF16), jnp.where(lo, zero, x).astype(BF16)

    rows = lambda sb: slice(sb * WINDOW, (sb + 1) * WINDOW)
    k_blocks = [kp_ref[...]] + [kc_ref[rows(sb), :] for sb in range(ATTN_SUB)]
    v_blocks = [vp_ref[...]] + [vc_ref[rows(sb), :] for sb in range(ATTN_SUB)]
    k_halves = [[halves(x, g) for x in k_blocks] for g in range(N_KV_HEADS)]
    v_halves = [[halves(x, g) for x in v_blocks] for g in range(N_KV_HEADS)]
    pairs = KV_REP // 2
    nt = (((1,), (1,)), ((), ()))

    for sb in range(ATTN_SUB):
        def softmax(s, head, sb=sb):
            s = jnp.where(valid[sb], s, NEG_INF)
            sink = sinks_ref[layer, head]
            m = jnp.maximum(jnp.max(s, axis=-1, keepdims=True), sink)
            p = jnp.exp(s - m)
            denom = jnp.sum(p, axis=-1, keepdims=True) + jnp.exp(sink - m)
            return (p / denom).astype(BF16)

        out_blocks = []
        for g in range(N_KV_HEADS):
            window = lambda hv, half: jnp.concatenate([hv[g][sb][half], hv[g][sb + 1][half]], axis=0)
            k_lo, k_hi, v_lo, v_hi = window(k_halves, 0), window(k_halves, 1), window(v_halves, 0), window(v_halves, 1)
            qg = jnp.concatenate([q_ref[rows(sb), (g * pairs + p) * LANES:(g * pairs + p + 1) * LANES]
                                  for p in range(pairs)], axis=0)
            s_even = lax.dot_general(qg, k_lo, nt, preferred_element_type=F32)
            s_odd = lax.dot_general(qg, k_hi, nt, preferred_element_type=F32)
            p_even = jnp.concatenate(
                [softmax(s_even[rows(p)], g * KV_REP + 2 * p) for p in range(pairs)], axis=0)
            p_odd = jnp.concatenate(
                [softmax(s_odd[rows(p)], g * KV_REP + 2 * p + 1) for p in range(pairs)], axis=0)
            o = (jnp.dot(p_even, v_lo, preferred_element_type=F32)
                 + jnp.dot(p_odd, v_hi, preferred_element_type=F32))
            out_blocks += [o[rows(p)] for p in range(pairs)]

        ssq = out_blocks[0] * out_blocks[0]
        for blk in out_blocks[1:]:
            ssq = ssq + blk * blk
        inv = lax.rsqrt(jnp.sum(ssq, axis=-1, keepdims=True) / ATTN_WIDTH + RMS_EPS)
        for i, blk in enumerate(out_blocks):
            cols = slice(i * LANES, (i + 1) * LANES)
            a_ref[rows(sb), cols] = (blk * inv * gn_ref[:, cols]).astype(BF16)


def _attn_prompt(l, q, k, v, sinks, gn, batch, seq):
    nb = seq // WINDOW
    steps = nb // ATTN_SUB
    cur = lambda b, j: (b * steps + j, 0)
    prev = lambda b, j: (b * nb + jnp.maximum(j * ATTN_SUB - 1, 0), 0)
    return pl.pallas_call(
        functools.partial(_attn_prompt_kernel, l),
        grid=(batch, steps),
        in_specs=[pl.BlockSpec(memory_space=pltpu.SMEM),
                  pl.BlockSpec((ATTN_SUB * WINDOW, ATTN_WIDTH), cur),
                  pl.BlockSpec((ATTN_SUB * WINDOW, KV_COLS), cur), pl.BlockSpec((WINDOW, KV_COLS), prev),
                  pl.BlockSpec((ATTN_SUB * WINDOW, KV_COLS), cur), pl.BlockSpec((WINDOW, KV_COLS), prev),
                  _of_layer(l, (1, ATTN_WIDTH))],
        out_specs=pl.BlockSpec((ATTN_SUB * WINDOW, ATTN_WIDTH), cur),
        out_shape=jax.ShapeDtypeStruct((batch * seq, ATTN_WIDTH), BF16),
        compiler_params=_params(("arbitrary", "arbitrary"), 32),
        name="attn_prompt",
    )(sinks, q, k, k, v, v, gn)


def _attn_sample_kernel(qbd_ref, kc_ref, vc_ref, kn_ref, vn_ref, sink_ref, o_ref):
    bb = qbd_ref.shape[0]
    rows, keys = bb * N_HEADS, bb * WINDOW
    q = qbd_ref[...].reshape(rows, LANES)
    kc = kc_ref[...].reshape(keys, KV_COLS).astype(BF16)
    vc = vc_ref[...].reshape(keys, KV_COLS).astype(BF16)
    per_head = lambda ref: jnp.broadcast_to(ref[...], (bb, N_HEADS, KV_COLS)).reshape(rows, KV_COLS)
    kn, vn = per_head(kn_ref), per_head(vn_ref)
    sink = jnp.concatenate([sink_ref[...]] * bb, axis=0)
    nt = (((1,), (1,)), ((), ()))
    s = lax.dot_general(q, kc, nt, preferred_element_type=F32)
    own = ((lax.broadcasted_iota(jnp.int32, s.shape, 0) >> _log2(N_HEADS))
           == (lax.broadcasted_iota(jnp.int32, s.shape, 1) >> _log2(WINDOW)))
    s = jnp.where(own, s, NEG_INF)
    s_new = jnp.sum(q.astype(F32) * kn, axis=-1, keepdims=True)
    m = jnp.maximum(jnp.maximum(jnp.max(s, axis=-1, keepdims=True), s_new), sink)
    p = jnp.exp(s - m)
    p_new = jnp.exp(s_new - m)
    denom = jnp.sum(p, axis=-1, keepdims=True) + p_new + jnp.exp(sink - m)
    o = jnp.dot((p / denom).astype(BF16), vc, preferred_element_type=F32) + (p_new / denom) * vn
    o_ref[...] = o.reshape(bb, N_HEADS, LANES)


def _attn_sample(l, qbd, kc, vc, kn, vn, sink_col, bb):
    nbatch = qbd.shape[0]
    blk = lambda *s: pl.BlockSpec((bb,) + s, lambda i: (i, 0, 0))
    cache = pl.BlockSpec((None, bb, WINDOW, KV_COLS), lambda i: (l, i, 0, 0))
    return pl.pallas_call(
        _attn_sample_kernel,
        grid=(nbatch // bb,),
        in_specs=[blk(N_HEADS, LANES), cache, cache,
                  blk(1, KV_COLS), blk(1, KV_COLS), _of_layer(l, (N_HEADS, 1))],
        out_specs=blk(N_HEADS, LANES),
        out_shape=jax.ShapeDtypeStruct((nbatch, N_HEADS, LANES), F32),
        compiler_params=_params(("arbitrary",), 32),
        name="attn_sample",
    )(qbd, kc, vc, kn, vn, sink_col)


def _rms_cast_kernel(x_ref, g_ref, o_ref):
    o_ref[...] = _rms(x_ref[...], g_ref[...]).astype(BF16)


def _rms_cast(l, x, g):
    n, d = x.shape
    return pl.pallas_call(
        _rms_cast_kernel,
        grid=(1,),
        in_specs=[_resident((n, d)), _of_layer(l, (1, d))],
        out_specs=pl.BlockSpec((n, d), lambda i: (0, 0)),
        out_shape=jax.ShapeDtypeStruct(x.shape, BF16),
        name="rms_cast",
    )(x, g)


def _glu_norm(z, wg_ref, bg_ref, gn_ref):
    gate = jax.nn.sigmoid(jnp.dot(z.astype(BF16), wg_ref[...], preferred_element_type=F32) + bg_ref[...])
    return _rms(z * gate, gn_ref[...])


def _ssm_operands(l, ssm, wg, bg, gn):
    layered = (ssm["bre"], ssm["bim"], ssm["cre"], ssm["cimn"], ssm["lre"], ssm["lim"], ssm["d"], wg, bg, gn)
    shared = (ssm["tile_b"], ssm["tile_c"])
    return (layered + shared,
            [_of_layer(l, a.shape[1:]) for a in layered] + [_resident(a.shape) for a in shared])


def _log2(n):
    assert n & (n - 1) == 0
    return n.bit_length() - 1


def _block_diag_scratch():
    return [pltpu.VMEM((N_CHUNK, CH_U, CH_S), BF16), pltpu.VMEM((N_CHUNK, CH_U, CH_S), BF16),
            pltpu.VMEM((N_CHUNK, CH_S, CH_U), BF16), pltpu.VMEM((N_CHUNK, CH_S, CH_U), BF16)]


def _expand_block_diag(t_ref, tile_ref, out_s, rows_per_group, cols_per_group):
    for c in range(N_CHUNK):
        tiled = jnp.dot(t_ref[c], tile_ref[...], preferred_element_type=F32)
        row_group = lax.broadcasted_iota(jnp.int32, tiled.shape, 0) >> _log2(rows_per_group)
        col_group = lax.broadcasted_iota(jnp.int32, tiled.shape, 1) >> _log2(cols_per_group)
        out_s[c] = jnp.where(row_group == col_group, tiled, 0.0).astype(BF16)


def _expand_projections(tb_re, tb_im, tc_re, tc_imn, tile_b, tile_c, bre_s, bim_s, cre_s, cimn_s):
    _expand_block_diag(tb_re, tile_b, bre_s, SSM_GROUP, SSM_STATE)
    _expand_block_diag(tb_im, tile_b, bim_s, SSM_GROUP, SSM_STATE)
    _expand_block_diag(tc_re, tile_c, cre_s, SSM_STATE, SSM_GROUP)
    _expand_block_diag(tc_imn, tile_c, cimn_s, SSM_STATE, SSM_GROUP)


def _ssm_prompt_kernel(u_ref, perm_ref, tb_re, tb_im, tc_re, tc_imn, lre_ref, lim_ref, d_ref,
                       wg_ref, bg_ref, gn_ref, tile_b, tile_c,
                       s_ref, hre_ref, him_ref,
                       bre_ref, bim_ref, cre_ref, cimn_ref,
                       xre_s, xim_s, hbre_s, hbim_s, pre_s, pim_s, cre_s, cim_s, hinre_s, hinim_s, y_s):
    b = pl.program_id(0)
    j = pl.program_id(1)

    @pl.when((b == 0) & (j == 0))
    def _():
        _expand_projections(tb_re, tb_im, tc_re, tc_imn, tile_b, tile_c, bre_ref, bim_ref, cre_ref, cimn_ref)
        lr, li = lre_ref[...], lim_ref[...]
        pr, pi = lr, li
        for i in range(SEG_LEN):
            if i:
                pr, pi = pr * lr - pi * li, pr * li + pi * lr
            pre_s[i * SUBLANES:(i + 1) * SUBLANES, :] = jnp.broadcast_to(pr, (SUBLANES, N_STATE))
            pim_s[i * SUBLANES:(i + 1) * SUBLANES, :] = jnp.broadcast_to(pi, (SUBLANES, N_STATE))

    @pl.when(j == 0)
    def _():
        cre_s[...] = jnp.zeros_like(cre_s)
        cim_s[...] = jnp.zeros_like(cim_s)

    u_nat = jnp.concatenate([u_ref[jb] for jb in range(U_BLOCKS)], axis=1)
    u_perm = jnp.dot(perm_ref[...], u_nat.astype(BF16), preferred_element_type=F32).astype(BF16)

    last = (SEG_LEN - 1) * SUBLANES
    for c in range(N_CHUNK):
        uc = u_perm[:, c * CH_U:(c + 1) * CH_U]
        chunk = slice(c * CH_S, (c + 1) * CH_S)
        xre_s[:, chunk] = jnp.dot(uc, bre_ref[c], preferred_element_type=F32)
        xim_s[:, chunk] = jnp.dot(uc, bim_ref[c], preferred_element_type=F32)

        for lc in range(CH_S // SCAN_LANES):
            sl = slice(c * CH_S + lc * SCAN_LANES, c * CH_S + (lc + 1) * SCAN_LANES)
            lr = jnp.broadcast_to(lre_ref[:, sl], (SUBLANES, SCAN_LANES))
            li = jnp.broadcast_to(lim_ref[:, sl], (SUBLANES, SCAN_LANES))
            hr = xre_s[0:SUBLANES, sl]
            hi = xim_s[0:SUBLANES, sl]
            for i in range(1, SEG_LEN):
                rows = slice(i * SUBLANES, (i + 1) * SUBLANES)
                hr, hi = (lr * hr - li * hi) + xre_s[rows, sl], (lr * hi + li * hr) + xim_s[rows, sl]
                xre_s[rows, sl] = hr
                xim_s[rows, sl] = hi

        l32r = pre_s[last:last + 1, chunk]
        l32i = pim_s[last:last + 1, chunk]
        cr, ci = cre_s[:, chunk], cim_s[:, chunk]
        for seg in range(SUBLANES):
            hinre_s[seg:seg + 1, chunk] = cr
            hinim_s[seg:seg + 1, chunk] = ci
            er = xre_s[last + seg:last + seg + 1, chunk]
            ei = xim_s[last + seg:last + seg + 1, chunk]
            cr, ci = (l32r * cr - l32i * ci) + er, (l32r * ci + l32i * cr) + ei
        cre_s[:, chunk] = cr
        cim_s[:, chunk] = ci

        for lc in range(CH_S // SCAN_LANES):
            sl = slice(c * CH_S + lc * SCAN_LANES, c * CH_S + (lc + 1) * SCAN_LANES)
            hr_in = jnp.concatenate([hinre_s[:, sl]] * 2, axis=0)
            hi_in = jnp.concatenate([hinim_s[:, sl]] * 2, axis=0)
            for i2 in range(SEG_LEN // 2):
                rows = slice(i2 * 2 * SUBLANES, (i2 + 1) * 2 * SUBLANES)
                pr, pi = pre_s[rows, sl], pim_s[rows, sl]
                hbre_s[rows, sl] = (xre_s[rows, sl] + (pr * hr_in - pi * hi_in)).astype(BF16)
                hbim_s[rows, sl] = (xim_s[rows, sl] + (pr * hi_in + pi * hr_in)).astype(BF16)

        y = (jnp.dot(hbre_s[:, chunk], cre_ref[c], preferred_element_type=F32)
             + jnp.dot(hbim_s[:, chunk], cimn_ref[c], preferred_element_type=F32))
        for h in range(BLOCKS_PER_CHUNK):
            y_s[c * BLOCKS_PER_CHUNK + h] = y[:, h * LANES:(h + 1) * LANES]

    y_nat = jnp.concatenate(
        [jnp.concatenate([y_s[jb, pl.ds(i0 * SUBLANES + seg, SUBLANES, stride=SUBLANES), :]
                          for seg in range(SUBLANES) for i0 in range(0, SEG_LEN, SUBLANES)], axis=0)
         for jb in range(U_BLOCKS)], axis=1)
    z = jax.nn.gelu(y_nat + d_ref[...] * u_nat)
    s_ref[...] = _glu_norm(z, wg_ref, bg_ref, gn_ref).astype(BF16)

    @pl.when(j == pl.num_programs(1) - 1)
    def _():
        hre_ref[...] = cre_s[...]
        him_ref[...] = cim_s[...]


def _ssm_prompt(l, u, perm, ssm, wg, bg, gn, batch, seq):
    nt = seq // SSM_TILE
    row = lambda b, j: (b * nt + j, 0)
    st = lambda b, j: (b, 0, 0)
    arrays, specs = _ssm_operands(l, ssm, wg, bg, gn)
    scratch = _block_diag_scratch() + [
        pltpu.VMEM((SSM_TILE, N_STATE), F32),
        pltpu.VMEM((SSM_TILE, N_STATE), F32),
        pltpu.VMEM((SSM_TILE, N_STATE), BF16),
        pltpu.VMEM((SSM_TILE, N_STATE), BF16),
        pltpu.VMEM((SSM_TILE, N_STATE), F32),
        pltpu.VMEM((SSM_TILE, N_STATE), F32),
        pltpu.VMEM((1, N_STATE), F32),
        pltpu.VMEM((1, N_STATE), F32),
        pltpu.VMEM((SUBLANES, N_STATE), F32),
        pltpu.VMEM((SUBLANES, N_STATE), F32),
        pltpu.VMEM((U_BLOCKS, SSM_TILE, LANES), F32),
    ]
    return pl.pallas_call(
        _ssm_prompt_kernel,
        grid=(batch, nt),
        in_specs=[pl.BlockSpec((U_BLOCKS, SSM_TILE, LANES), lambda b, j: (0, b * nt + j, 0)),
                  _resident((SSM_TILE, SSM_TILE))] + specs,
        out_specs=[pl.BlockSpec((SSM_TILE, SSM_WIDTH), row),
                   pl.BlockSpec((None, 1, N_STATE), st),
                   pl.BlockSpec((None, 1, N_STATE), st)],
        out_shape=[jax.ShapeDtypeStruct((batch * seq, SSM_WIDTH), BF16),
                   jax.ShapeDtypeStruct((batch, 1, N_STATE), F32),
                   jax.ShapeDtypeStruct((batch, 1, N_STATE), F32)],
        scratch_shapes=scratch,
        compiler_params=_params(("arbitrary", "arbitrary"), 52),
        name="ssm_prompt",
    )(u, perm, *arrays)


def _ssm_sample_kernel(u_ref, h0re_ref, h0im_ref, tb_re, tb_im, tc_re, tc_imn, lre_ref, lim_ref, d_ref,
                       wg_ref, bg_ref, gn_ref, tile_b, tile_c, s_ref, hre_ref, him_ref,
                       bre_ref, bim_ref, cre_ref, cimn_ref):
    _expand_projections(tb_re, tb_im, tc_re, tc_imn, tile_b, tile_c, bre_ref, bim_ref, cre_ref, cimn_ref)
    u = jnp.concatenate([u_ref[jb] for jb in range(U_BLOCKS)], axis=1)
    zs = []
    for c in range(N_CHUNK):
        cs = slice(c * CH_U, (c + 1) * CH_U)
        ss = slice(c * CH_S, (c + 1) * CH_S)
        uc = u[:, cs].astype(BF16)
        lr, li = lre_ref[:, ss], lim_ref[:, ss]
        h0r, h0i = h0re_ref[:, ss], h0im_ref[:, ss]
        hr = jnp.dot(uc, bre_ref[c], preferred_element_type=F32) + (lr * h0r - li * h0i)
        hi = jnp.dot(uc, bim_ref[c], preferred_element_type=F32) + (lr * h0i + li * h0r)
        hre_ref[:, ss] = hr
        him_ref[:, ss] = hi
        y = (jnp.dot(hr.astype(BF16), cre_ref[c], preferred_element_type=F32)
             + jnp.dot(hi.astype(BF16), cimn_ref[c], preferred_element_type=F32))
        zs.append(jax.nn.gelu(y + d_ref[:, cs] * u[:, cs]))
    z = jnp.concatenate(zs, axis=1)
    s_ref[...] = _glu_norm(z, wg_ref, bg_ref, gn_ref).astype(BF16)


def _ssm_sample(l, u, h0re, h0im, ssm, wg, bg, gn):
    n = u.shape[1]
    arrays, specs = _ssm_operands(l, ssm, wg, bg, gn)
    whole = lambda *s: pl.BlockSpec(s, lambda i: (0,) * len(s))
    return pl.pallas_call(
        _ssm_sample_kernel,
        grid=(1,),
        in_specs=[_resident((U_BLOCKS, n, LANES)), _of_layer(l, (n, N_STATE)), _of_layer(l, (n, N_STATE))] + specs,
        out_specs=[whole(n, SSM_WIDTH), whole(n, N_STATE), whole(n, N_STATE)],
        out_shape=[jax.ShapeDtypeStruct((n, SSM_WIDTH), BF16),
                   jax.ShapeDtypeStruct((n, N_STATE), F32),
                   jax.ShapeDtypeStruct((n, N_STATE), F32)],
        scratch_shapes=_block_diag_scratch(),
        compiler_params=_params(("arbitrary",), 40),
        name="ssm_sample",
    )(u, h0re, h0im, *arrays)


def _out_proj_kernel(x_ref, a_ref, s_ref, w_ref, o_ref):
    acc = (jnp.dot(a_ref[...], w_ref[0:ATTN_WIDTH, :], preferred_element_type=F32)
           + jnp.dot(s_ref[...], w_ref[ATTN_WIDTH:, :], preferred_element_type=F32))
    o_ref[...] = x_ref[...] + acc


def _out_proj(l, x, a, s, w, tm):
    m = x.shape[0]
    row = lambda i: (i, 0)
    return pl.pallas_call(
        _out_proj_kernel,
        grid=(m // tm,),
        in_specs=[pl.BlockSpec((tm, D_MODEL), row),
                  pl.BlockSpec((tm, ATTN_WIDTH), row),
                  pl.BlockSpec((tm, SSM_WIDTH), row),
                  _of_layer(l, (D_MODEL, D_MODEL))],
        out_specs=pl.BlockSpec((tm, D_MODEL), row),
        out_shape=jax.ShapeDtypeStruct((m, D_MODEL), F32),
        compiler_params=_params(("arbitrary",), 48),
        name="out_proj",
    )(x, a, s, w)


FF_TILE = 512
N_FF = D_FF // FF_TILE
FF_SUB = MXU_DIM
CONV_PAD = SUBLANES


def _ffn_weight_specs(l, idx):
    gcol = lambda *g: (l, 0, idx(*g))
    vcol = lambda *g: (l, 0, N_FF + idx(*g))
    return [pl.BlockSpec((None, D_MODEL, FF_TILE), gcol), pl.BlockSpec((None, D_MODEL, FF_TILE), vcol),
            pl.BlockSpec((None, CONV_W, FF_TILE), gcol), pl.BlockSpec((None, CONV_W, FF_TILE), vcol),
            pl.BlockSpec((None, 1, FF_TILE), gcol), pl.BlockSpec((None, 1, FF_TILE), vcol),
            pl.BlockSpec((None, FF_TILE, D_MODEL), lambda *g: (l, idx(*g), 0))]


def _ffn_prompt_kernel(final_norm, tiles_per_seq,
                       x_ref, gn_ref, fg_ref, wug_ref, wuv_ref, cwg_ref, cwv_ref, cbg_ref, cbv_ref, wd_ref,
                       o_ref, cg_ref, cv_ref,
                       h_s, extg_s, extv_s, carryg_s, carryv_s):
    m = pl.program_id(0)
    f = pl.program_id(1)
    tm = x_ref.shape[0]

    @pl.when(f == 0)
    def _():
        x = x_ref[...]
        h_s[...] = _rms(x, gn_ref[...]).astype(BF16)
        o_ref[...] = x

    seq_start = (m % tiles_per_seq) == 0

    @pl.when(seq_start)
    def _():
        extg_s[0:CONV_PAD, :] = jnp.zeros((CONV_PAD, FF_TILE), F32)
        extv_s[0:CONV_PAD, :] = jnp.zeros((CONV_PAD, FF_TILE), F32)

    @pl.when(jnp.logical_not(seq_start))
    def _():
        extg_s[0:CONV_PAD, :] = carryg_s[f]
        extv_s[0:CONV_PAD, :] = carryv_s[f]

    subs = [slice(s * FF_SUB, (s + 1) * FF_SUB) for s in range(FF_TILE // FF_SUB)]
    for cs in subs:
        extg_s[CONV_PAD:, cs] = jnp.dot(h_s[...], wug_ref[:, cs], preferred_element_type=F32)
        extv_s[CONV_PAD:, cs] = jnp.dot(h_s[...], wuv_ref[:, cs], preferred_element_type=F32)

    def conv(ext_s, cw_ref, cb_ref, cs):
        out = cb_ref[:, cs]
        for t in range(CONV_W):
            lag = CONV_W - 1 - t
            out = out + cw_ref[t:t + 1, cs] * ext_s[CONV_PAD - lag:CONV_PAD - lag + tm, cs]
        return out

    down = None
    for cs in subs:
        act = (jax.nn.silu(conv(extg_s, cwg_ref, cbg_ref, cs)) * conv(extv_s, cwv_ref, cbv_ref, cs)).astype(BF16)
        part = jnp.dot(act, wd_ref[cs, :], preferred_element_type=F32)
        down = part if down is None else down + part
    o_ref[...] += down

    carryg_s[f] = extg_s[tm:tm + CONV_PAD, :]
    carryv_s[f] = extv_s[tm:tm + CONV_PAD, :]
    cg_ref[f] = extg_s[tm + CONV_PAD - (CONV_W - 1):tm + CONV_PAD, :]
    cv_ref[f] = extv_s[tm + CONV_PAD - (CONV_W - 1):tm + CONV_PAD, :]

    if final_norm:
        @pl.when(f == pl.num_programs(1) - 1)
        def _():
            o_ref[...] = _rms(o_ref[...], fg_ref[...])


def _ffn_prompt(l, x, gn, wu, cw, cb, wd, fg, batch, seq, tm, final_norm):
    m = batch * seq
    tiles_per_seq = seq // tm
    row = lambda i, f: (i, 0)
    state = lambda i, f: (i // tiles_per_seq, 0, 0, 0)
    return pl.pallas_call(
        functools.partial(_ffn_prompt_kernel, final_norm, tiles_per_seq),
        grid=(m // tm, N_FF),
        in_specs=[pl.BlockSpec((tm, D_MODEL), row, pipeline_mode=pl.Buffered(1)),
                  _of_layer(l, (1, D_MODEL)), _resident((1, D_MODEL))]
                 + _ffn_weight_specs(l, lambda i, f: f),
        out_specs=[pl.BlockSpec((tm, D_MODEL), row),
                   pl.BlockSpec((None, N_FF, CONV_W - 1, FF_TILE), state),
                   pl.BlockSpec((None, N_FF, CONV_W - 1, FF_TILE), state)],
        out_shape=[jax.ShapeDtypeStruct((m, D_MODEL), F32),
                   jax.ShapeDtypeStruct((batch, N_FF, CONV_W - 1, FF_TILE), F32),
                   jax.ShapeDtypeStruct((batch, N_FF, CONV_W - 1, FF_TILE), F32)],
        scratch_shapes=[pltpu.VMEM((tm, D_MODEL), BF16),
                        pltpu.VMEM((tm + CONV_PAD, FF_TILE), F32),
                        pltpu.VMEM((tm + CONV_PAD, FF_TILE), F32),
                        pltpu.VMEM((N_FF, CONV_PAD, FF_TILE), F32),
                        pltpu.VMEM((N_FF, CONV_PAD, FF_TILE), F32)],
        compiler_params=_params(("arbitrary", "arbitrary"), 56),
        name="ffn_prompt",
    )(x, gn, fg, wu, wu, cw, cw, cb, cb, wd)


def _ffn_sample_kernel(final_norm,
                       x_ref, gn_ref, fg_ref, wug_ref, wuv_ref, cwg_ref, cwv_ref, cbg_ref, cbv_ref, wd_ref,
                       s0g_ref, s0v_ref, s1g_ref, s1v_ref,
                       o_ref, ug_ref, uv_ref, h_s):
    f = pl.program_id(0)

    @pl.when(f == 0)
    def _():
        x = x_ref[...]
        h_s[...] = _rms(x, gn_ref[...]).astype(BF16)
        o_ref[...] = x

    def conv(w_ref, cw_ref, cb_ref, s0_ref, s1_ref, up_ref):
        up = jnp.dot(h_s[...], w_ref[...], preferred_element_type=F32)
        up_ref[...] = up
        return ((cb_ref[...] + cw_ref[0:1, :] * s0_ref[...]) + cw_ref[1:2, :] * s1_ref[...]) + cw_ref[2:3, :] * up

    gate = conv(wug_ref, cwg_ref, cbg_ref, s0g_ref, s1g_ref, ug_ref)
    val = conv(wuv_ref, cwv_ref, cbv_ref, s0v_ref, s1v_ref, uv_ref)
    act = (jax.nn.silu(gate) * val).astype(BF16)
    o_ref[...] += jnp.dot(act, wd_ref[...], preferred_element_type=F32)

    if final_norm:
        @pl.when(f == pl.num_programs(0) - 1)
        def _():
            o_ref[...] = _rms(o_ref[...], fg_ref[...])


def _ffn_sample(l, x, gn, wu, cw, cb, wd, fg, s0, s1, final_norm):
    n = x.shape[0]
    gcol = lambda f: (l, 0, f)
    vcol = lambda f: (l, 0, N_FF + f)
    st = lambda im: pl.BlockSpec((None, n, FF_TILE), im)
    return pl.pallas_call(
        functools.partial(_ffn_sample_kernel, final_norm),
        grid=(N_FF,),
        in_specs=[_resident((n, D_MODEL)), _of_layer(l, (1, D_MODEL)), _resident((1, D_MODEL))]
                 + _ffn_weight_specs(l, lambda f: f)
                 + [st(gcol), st(vcol), st(gcol), st(vcol)],
        out_specs=[pl.BlockSpec((n, D_MODEL), lambda f: (0, 0)),
                   pl.BlockSpec((n, FF_TILE), lambda f: (0, f)),
                   pl.BlockSpec((n, FF_TILE), lambda f: (0, f))],
        out_shape=[jax.ShapeDtypeStruct((n, D_MODEL), F32),
                   jax.ShapeDtypeStruct((n, D_FF), F32),
                   jax.ShapeDtypeStruct((n, D_FF), F32)],
        scratch_shapes=[pltpu.VMEM((n, D_MODEL), BF16)],
        compiler_params=_params(("arbitrary",), 32),
        name="ffn_sample",
    )(x, gn, fg, wu, wu, cw, cw, cb, cb, wd, s0, s0, s1, s1)


def _rope_tables(pos):
    half = HEAD_DIM // 2
    inv = ROPE_THETA ** (-jnp.arange(half, dtype=F32) / half)
    ang = pos.astype(F32)[:, None] * inv[None, :]
    cos, sin = jnp.cos(ang), jnp.sin(ang)
    reps = LANES // HEAD_DIM
    return (jnp.concatenate([cos, cos] * reps, axis=1),
            jnp.concatenate([-sin, sin] * reps, axis=1))


def _ssm_params(a_re, a_im, b_re, b_im, c_re, c_im, d, log_dt):
    depth = a_re.shape[0]
    dt = jnp.exp(log_dt)[..., None]
    mag = jnp.exp(a_re * dt)
    lr, li = mag * jnp.cos(a_im * dt), mag * jnp.sin(a_im * dt)
    nr, ni = lr - 1.0, li
    den = a_re * a_re + a_im * a_im
    qr = (nr * a_re + ni * a_im) / den
    qi = (ni * a_re - nr * a_im) / den
    bbr = qr[..., None] * b_re - qi[..., None] * b_im
    bbi = qr[..., None] * b_im + qi[..., None] * b_re

    def compact(t):
        r, c = t.shape[2], t.shape[3]
        t = t.reshape(depth, N_CHUNK, GROUP_CHUNK * r, c).astype(BF16)
        return jnp.pad(t, ((0, 0), (0, 0), (0, 0), (0, LANES - c)))

    def lane_tiling(c):
        src = jnp.arange(LANES)[:, None]
        dst = jnp.arange(GROUP_CHUNK * c)[None, :] % c
        return (src == dst).astype(BF16)

    return dict(bre=compact(bbr.transpose(0, 1, 3, 2)), bim=compact(bbi.transpose(0, 1, 3, 2)),
                cre=compact(c_re.transpose(0, 1, 3, 2)), cimn=compact((-c_im).transpose(0, 1, 3, 2)),
                tile_b=lane_tiling(SSM_STATE), tile_c=lane_tiling(SSM_GROUP),
                lre=lr.reshape(depth, 1, N_STATE), lim=li.reshape(depth, 1, N_STATE),
                d=d.reshape(depth, 1, SSM_WIDTH))


def _segment_permutation():
    r = jnp.arange(SSM_TILE)
    src = (r % SUBLANES) * SEG_LEN + r // SUBLANES
    return (src[:, None] == jnp.arange(SSM_TILE)[None, :]).astype(BF16)


def kernel(x_prompt, x_sample, cache_k, cache_v, state_ssm_re, state_ssm_im, state_conv, attn_norm_g, w_in, attn_sinks, ssm_a_re, ssm_a_im, ssm_b_re, ssm_b_im, ssm_c_re, ssm_c_im, ssm_d, ssm_log_dt, w_glu, b_glu, attn_out_norm_g, ssm_out_norm_g, w_out, ffn_norm_g, w_up, conv_w, conv_b, w_down, final_norm_g):
    batch, seq, _ = x_prompt.shape
    nsamp, dec_seq, _ = x_sample.shape
    wbuf = cache_k.shape[2]
    assert dec_seq == 1 and wbuf == WINDOW and seq % SSM_TILE == 0
    assert PAST_LEN >= wbuf

    w_in_b, w_glu_b, w_out_b = w_in.astype(BF16), w_glu.astype(BF16), w_out.astype(BF16)
    w_up_b, w_down_b = w_up.astype(BF16), w_down.astype(BF16)
    ssm = _ssm_params(ssm_a_re, ssm_a_im, ssm_b_re, ssm_b_im, ssm_c_re, ssm_c_im, ssm_d, ssm_log_dt)
    perm = _segment_permutation()

    cos_p, sin_p = _rope_tables(jnp.arange(seq, dtype=jnp.int32))
    cos_s, sin_s = _rope_tables(jnp.full((nsamp,), PAST_LEN, dtype=jnp.int32))

    rows = lambda a: a.reshape(a.shape[0], 1, a.shape[-1])
    g_in, g_a, g_s, g_f = rows(attn_norm_g), rows(attn_out_norm_g), rows(ssm_out_norm_g), rows(ffn_norm_g)
    bg, cb = rows(b_glu), rows(conv_b)
    fg = final_norm_g.reshape(1, D_MODEL)
    sink_col = attn_sinks.reshape(DEPTH, N_HEADS, 1)
    head_is_lo = (jnp.arange(N_HEADS) < KV_REP)[None, :, None]
    kc = cache_k.reshape(DEPTH, nsamp, wbuf, KV_COLS)
    vc = cache_v.reshape(DEPTH, nsamp, wbuf, KV_COLS)
    h0re = state_ssm_re.reshape(DEPTH, nsamp, N_STATE)
    h0im = state_ssm_im.reshape(DEPTH, nsamp, N_STATE)
    conv0, conv1 = state_conv[:, :, 0, :], state_conv[:, :, 1, :]

    xp = x_prompt.reshape(batch * seq, D_MODEL)
    xs = x_sample.reshape(nsamp, D_MODEL)
    tm_p = 512
    tm_ffn = 1024
    outs = {k: [] for k in ("kp", "vp", "hrp", "hip", "cgp", "cvp", "ks", "vs", "hrs", "his", "ugs", "uvs")}

    for l in range(DEPTH):
        last = l == DEPTH - 1

        q, k, v, u = _in_proj(l, xp, g_in, w_in_b, cos_p, sin_p, tm_p)
        a = _attn_prompt(l, q, k, v, attn_sinks, g_a, batch, seq)
        s, hre, him = _ssm_prompt(l, u, perm, ssm, w_glu_b, bg, g_s, batch, seq)
        x1 = _out_proj(l, xp, a, s, w_out_b, tm_p)
        xp, cg, cv = _ffn_prompt(l, x1, g_f, w_up_b, conv_w, cb, w_down_b, fg, batch, seq, tm_ffn, last)
        outs["kp"].append(k.reshape(batch, seq, KV_COLS)[:, seq - WINDOW:])
        outs["vp"].append(v.reshape(batch, seq, KV_COLS)[:, seq - WINDOW:])
        outs["hrp"].append(hre)
        outs["hip"].append(him)
        outs["cgp"].append(cg)
        outs["cvp"].append(cv)

        q, k, v, u = _in_proj(l, xs, g_in, w_in_b, cos_s, sin_s, nsamp)
        qh = q.reshape(nsamp, N_HEADS, HEAD_DIM)
        zq = jnp.zeros_like(qh)
        qbd = jnp.where(head_is_lo, jnp.concatenate([qh, zq], axis=-1), jnp.concatenate([zq, qh], axis=-1))
        o = _attn_sample(l, qbd, kc, vc, k.reshape(nsamp, 1, KV_COLS), v.reshape(nsamp, 1, KV_COLS), sink_col, 8)
        a_raw = jnp.concatenate([o[:, :KV_REP, :HEAD_DIM].reshape(nsamp, -1),
                                 o[:, KV_REP:, HEAD_DIM:].reshape(nsamp, -1)], axis=1)
        a = _rms_cast(l, a_raw, g_a)
        s, hre, him = _ssm_sample(l, u, h0re, h0im, ssm, w_glu_b, bg, g_s)
        x1 = _out_proj(l, xs, a, s, w_out_b, nsamp)
        xs, ug, uv = _ffn_sample(l, x1, g_f, w_up_b, conv_w, cb, w_down_b, fg, conv0, conv1, last)
        outs["ks"].append(k)
        outs["vs"].append(v)
        outs["hrs"].append(hre)
        outs["his"].append(him)
        outs["ugs"].append(ug)
        outs["uvs"].append(uv)

    st = lambda name: jnp.stack(outs[name], axis=0)
    heads = lambda t: t.reshape(t.shape[:-1] + (N_KV_HEADS, HEAD_DIM))
    states = lambda t: t.reshape(DEPTH, -1, SSM_GROUPS, SSM_STATE)
    unblock = lambda t: t.transpose(0, 1, 3, 2, 4).reshape(DEPTH, batch, CONV_W - 1, D_FF)
    conv_prompt = jnp.concatenate([unblock(st("cgp")), unblock(st("cvp"))], axis=-1)
    k_sample = jnp.concatenate([cache_k[:, :, 1:], heads(st("ks"))[:, :, None]], axis=2)
    v_sample = jnp.concatenate([cache_v[:, :, 1:], heads(st("vs"))[:, :, None]], axis=2)
    conv_sample = jnp.stack([conv1, jnp.concatenate([st("ugs"), st("uvs")], axis=-1)], axis=2)
    return (xp.reshape(batch, seq, D_MODEL), xs.reshape(nsamp, 1, D_MODEL),
            heads(st("kp")), heads(st("vp")), states(st("hrp")), states(st("hip")), conv_prompt,
            k_sample, v_sample, states(st("hrs")), states(st("his")), conv_sample)
```

```python
import functools

import jax
import jax.numpy as jnp
from jax import lax
from jax.experimental import pallas as pl
from jax.experimental.pallas import tpu as pltpu

F32 = jnp.float32
BF16 = jnp.bfloat16

D_MODEL = 2048
DEPTH = 4
ATTN_WIDTH = 1024
SSM_WIDTH = 1024
HEAD_DIM = 64
N_HEADS = 16
N_KV_HEADS = 2
KV_REP = 8
KV_COLS = N_KV_HEADS * HEAD_DIM
WINDOW = 128
ROPE_THETA = 10000.0
SSM_GROUP = 16
SSM_GROUPS = 64
SSM_STATE = 64
N_STATE = SSM_GROUPS * SSM_STATE
D_FF = 5632
CONV_W = 3
RMS_EPS = 1e-6
IN_COLS = ATTN_WIDTH + 2 * KV_COLS + SSM_WIDTH
NEG_INF = -1e30
PAST_LEN = 16384
SCORE_SCALE = HEAD_DIM ** -0.5
assert SCORE_SCALE == 2.0 ** -3

LANES = 128
SUBLANES = 8
MXU_DIM = 256
MIB = 1024 * 1024
U_BLOCKS = SSM_WIDTH // LANES

GROUP_CHUNK = 16
N_CHUNK = SSM_GROUPS // GROUP_CHUNK
CH_U = GROUP_CHUNK * SSM_GROUP
CH_S = GROUP_CHUNK * SSM_STATE
BLOCKS_PER_CHUNK = CH_U // LANES

SEG_LEN = 32
SSM_TILE = SUBLANES * SEG_LEN
SCAN_LANES = 512


def _rms(x, g):
    ms = jnp.mean(x * x, axis=-1, keepdims=True)
    return x * lax.rsqrt(ms + RMS_EPS) * g


def _params(sem, vmem_mib):
    return pltpu.CompilerParams(dimension_semantics=sem, vmem_limit_bytes=vmem_mib * MIB)


def _resident(shape):
    nd = len(shape)
    return pl.BlockSpec(shape, lambda *_: (0,) * nd, pipeline_mode=pl.Buffered(1))


def _of_layer(l, shape):
    nd = len(shape)
    return pl.BlockSpec((None,) + tuple(shape), lambda *_: (l,) + (0,) * nd, pipeline_mode=pl.Buffered(1))


def _in_proj_kernel(x_ref, g_ref, w_ref, cos_ref, sin_ref, q_ref, k_ref, v_ref, u_ref):
    h = _rms(x_ref[...], g_ref[...]).astype(BF16)
    proj = jnp.dot(h, w_ref[...], preferred_element_type=F32)
    cos = cos_ref[...]
    sin = sin_ref[...]
    lane = lax.broadcasted_iota(jnp.int32, cos.shape, 1)
    first_half = (lane & (HEAD_DIM - 1)) < (HEAD_DIM // 2)

    def rope(blk):
        partner = jnp.where(first_half,
                            pltpu.roll(blk, LANES - HEAD_DIM // 2, 1),
                            pltpu.roll(blk, HEAD_DIM // 2, 1))
        return blk * cos + partner * sin

    for j in range(ATTN_WIDTH // LANES):
        q_ref[:, j * LANES:(j + 1) * LANES] = (rope(proj[:, j * LANES:(j + 1) * LANES]) * SCORE_SCALE).astype(BF16)
    k_ref[...] = rope(proj[:, ATTN_WIDTH:ATTN_WIDTH + KV_COLS])
    v_ref[...] = proj[:, ATTN_WIDTH + KV_COLS:ATTN_WIDTH + 2 * KV_COLS]
    u0 = ATTN_WIDTH + 2 * KV_COLS
    for j in range(U_BLOCKS):
        u_ref[j] = proj[:, u0 + j * LANES:u0 + (j + 1) * LANES]


def _in_proj(l, x, g, w, cos, sin, tm):
    m = x.shape[0]
    pos_tiles = cos.shape[0] // tm
    row = lambda i: (i, 0)
    return pl.pallas_call(
        _in_proj_kernel,
        grid=(m // tm,),
        in_specs=[pl.BlockSpec((tm, D_MODEL), row),
                  _of_layer(l, (1, D_MODEL)),
                  _of_layer(l, (D_MODEL, IN_COLS)),
                  pl.BlockSpec((tm, LANES), lambda i: (i % pos_tiles, 0)),
                  pl.BlockSpec((tm, LANES), lambda i: (i % pos_tiles, 0))],
        out_specs=[pl.BlockSpec((tm, ATTN_WIDTH), row),
                   pl.BlockSpec((tm, KV_COLS), row),
                   pl.BlockSpec((tm, KV_COLS), row),
                   pl.BlockSpec((U_BLOCKS, tm, LANES), lambda i: (0, i, 0))],
        out_shape=[jax.ShapeDtypeStruct((m, ATTN_WIDTH), BF16),
                   jax.ShapeDtypeStruct((m, KV_COLS), F32),
                   jax.ShapeDtypeStruct((m, KV_COLS), F32),
                   jax.ShapeDtypeStruct((U_BLOCKS, m, LANES), F32)],
        compiler_params=_params(("arbitrary",), 48),
        name="in_proj",
    )(x, g, w, cos, sin)


ATTN_SUB = 4


def _attn_prompt_kernel(layer, sinks_ref, q_ref, kc_ref, kp_ref, vc_ref, vp_ref, gn_ref, a_ref):
    j = pl.program_id(1)
    lane = lax.broadcasted_iota(jnp.int32, (WINDOW, LANES), 1)
    lo = lane < HEAD_DIM
    qi = lax.broadcasted_iota(jnp.int32, (WINDOW, 2 * WINDOW), 0)
    kj = lax.broadcasted_iota(jnp.int32, (WINDOW, 2 * WINDOW), 1)
    diff = qi + WINDOW - kj
    band = (diff >= 0) & (diff <= WINDOW)
    first_key = jnp.where(j > 0, 0, WINDOW)
    valid = [band & (kj >= first_key)] + [band] * (ATTN_SUB - 1)
    zero = jnp.zeros((WINDOW, LANES), F32)

    def halves(x, g):
        r = pltpu.roll(x, HEAD_DIM, 1)
        if g == 0:
            return jnp.where(lo, x, zero).astype(BF16), jnp.where(lo, zero, r).astype(BF16)
        return jnp.where(lo, r, zero).astype(BF16), jnp.where(lo, zero, x).astype(BF16)

    rows = lambda sb: slice(sb * WINDOW, (sb + 1) * WINDOW)
    k_blocks = [kp_ref[...]] + [kc_ref[rows(sb), :] for sb in range(ATTN_SUB)]
    v_blocks = [vp_ref[...]] + [vc_ref[rows(sb), :] for sb in range(ATTN_SUB)]
    k_halves = [[halves(x, g) for x in k_blocks] for g in range(N_KV_HEADS)]
    v_halves = [[halves(x, g) for x in v_blocks] for g in range(N_KV_HEADS)]
    pairs = KV_REP // 2
    nt = (((1,), (1,)), ((), ()))

    for sb in range(ATTN_SUB):
        def softmax(s, head, sb=sb):
            s = jnp.where(valid[sb], s, NEG_INF)
            sink = sinks_ref[layer, head]
            m = jnp.maximum(jnp.max(s, axis=-1, keepdims=True), sink)
            p = jnp.exp(s - m)
            denom = jnp.sum(p, axis=-1, keepdims=True) + jnp.exp(sink - m)
            return (p / denom).astype(BF16)

        out_blocks = []
        for g in range(N_KV_HEADS):
            window = lambda hv, half: jnp.concatenate([hv[g][sb][half], hv[g][sb + 1][half]], axis=0)
            k_lo, k_hi, v_lo, v_hi = window(k_halves, 0), window(k_halves, 1), window(v_halves, 0), window(v_halves, 1)
            qg = jnp.concatenate([q_ref[rows(sb), (g * pairs + p) * LANES:(g * pairs + p + 1) * LANES]
                                  for p in range(pairs)], axis=0)
            s_even = lax.dot_general(qg, k_lo, nt, preferred_element_type=F32)
            s_odd = lax.dot_general(qg, k_hi, nt, preferred_element_type=F32)
            p_even = jnp.concatenate(
                [softmax(s_even[rows(p)], g * KV_REP + 2 * p) for p in range(pairs)], axis=0)
            p_odd = jnp.concatenate(
                [softmax(s_odd[rows(p)], g * KV_REP + 2 * p + 1) for p in range(pairs)], axis=0)
            o = (jnp.dot(p_even, v_lo, preferred_element_type=F32)
                 + jnp.dot(p_odd, v_hi, preferred_element_type=F32))
            out_blocks += [o[rows(p)] for p in range(pairs)]

        ssq = out_blocks[0] * out_blocks[0]
        for blk in out_blocks[1:]:
            ssq = ssq + blk * blk
        inv = lax.rsqrt(jnp.sum(ssq, axis=-1, keepdims=True) / ATTN_WIDTH + RMS_EPS)
        for i, blk in enumerate(out_blocks):
            cols = slice(i * LANES, (i + 1) * LANES)
            a_ref[rows(sb), cols] = (blk * inv * gn_ref[:, cols]).astype(BF16)


def _attn_prompt(l, q, k, v, sinks, gn, batch, seq):
    nb = seq // WINDOW
    steps = nb // ATTN_SUB
    cur = lambda b, j: (b * steps + j, 0)
    prev = lambda b, j: (b * nb + jnp.maximum(j * ATTN_SUB - 1, 0), 0)
    return pl.pallas_call(
        functools.partial(_attn_prompt_kernel, l),
        grid=(batch, steps),
        in_specs=[pl.BlockSpec(memory_space=pltpu.SMEM),
                  pl.BlockSpec((ATTN_SUB * WINDOW, ATTN_WIDTH), cur),
                  pl.BlockSpec((ATTN_SUB * WINDOW, KV_COLS), cur), pl.BlockSpec((WINDOW, KV_COLS), prev),
                  pl.BlockSpec((ATTN_SUB * WINDOW, KV_COLS), cur), pl.BlockSpec((WINDOW, KV_COLS), prev),
                  _of_layer(l, (1, ATTN_WIDTH))],
        out_specs=pl.BlockSpec((ATTN_SUB * WINDOW, ATTN_WIDTH), cur),
        out_shape=jax.ShapeDtypeStruct((batch * seq, ATTN_WIDTH), BF16),
        compiler_params=_params(("arbitrary", "arbitrary"), 32),
        name="attn_prompt",
    )(sinks, q, k, k, v, v, gn)


def _attn_sample_kernel(qbd_ref, kc_ref, vc_ref, kn_ref, vn_ref, sink_ref, o_ref):
    bb = qbd_ref.shape[0]
    rows, keys = bb * N_HEADS, bb * WINDOW
    q = qbd_ref[...].reshape(rows, LANES)
    kc = kc_ref[...].reshape(keys, KV_COLS).astype(BF16)
    vc = vc_ref[...].reshape(keys, KV_COLS).astype(BF16)
    per_head = lambda ref: jnp.broadcast_to(ref[...], (bb, N_HEADS, KV_COLS)).reshape(rows, KV_COLS)
    kn, vn = per_head(kn_ref), per_head(vn_ref)
    sink = jnp.concatenate([sink_ref[...]] * bb, axis=0)
    nt = (((1,), (1,)), ((), ()))
    s = lax.dot_general(q, kc, nt, preferred_element_type=F32)
    own = ((lax.broadcasted_iota(jnp.int32, s.shape, 0) >> _log2(N_HEADS))
           == (lax.broadcasted_iota(jnp.int32, s.shape, 1) >> _log2(WINDOW)))
    s = jnp.where(own, s, NEG_INF)
    s_new = jnp.sum(q.astype(F32) * kn, axis=-1, keepdims=True)
    m = jnp.maximum(jnp.maximum(jnp.max(s, axis=-1, keepdims=True), s_new), sink)
    p = jnp.exp(s - m)
    p_new = jnp.exp(s_new - m)
    denom = jnp.sum(p, axis=-1, keepdims=True) + p_new + jnp.exp(sink - m)
    o = jnp.dot((p / denom).astype(BF16), vc, preferred_element_type=F32) + (p_new / denom) * vn
    o_ref[...] = o.reshape(bb, N_HEADS, LANES)


def _attn_sample(l, qbd, kc, vc, kn, vn, sink_col, bb):
    nbatch = qbd.shape[0]
    blk = lambda *s: pl.BlockSpec((bb,) + s, lambda i: (i, 0, 0))
    cache = pl.BlockSpec((None, bb, WINDOW, KV_COLS), lambda i: (l, i, 0, 0))
    return pl.pallas_call(
        _attn_sample_kernel,
        grid=(nbatch // bb,),
        in_specs=[blk(N_HEADS, LANES), cache, cache,
                  blk(1, KV_COLS), blk(1, KV_COLS), _of_layer(l, (N_HEADS, 1))],
        out_specs=blk(N_HEADS, LANES),
        out_shape=jax.ShapeDtypeStruct((nbatch, N_HEADS, LANES), F32),
        compiler_params=_params(("arbitrary",), 32),
        name="attn_sample",
    )(qbd, kc, vc, kn, vn, sink_col)


def _rms_cast_kernel(x_ref, g_ref, o_ref):
    o_ref[...] = _rms(x_ref[...], g_ref[...]).astype(BF16)


def _rms_cast(l, x, g):
    n, d = x.shape
    return pl.pallas_call(
        _rms_cast_kernel,
        grid=(1,),
        in_specs=[_resident((n, d)), _of_layer(l, (1, d))],
        out_specs=pl.BlockSpec((n, d), lambda i: (0, 0)),
        out_shape=jax.ShapeDtypeStruct(x.shape, BF16),
        name="rms_cast",
    )(x, g)


def _glu_norm(z, wg_ref, bg_ref, gn_ref):
    gate = jax.nn.sigmoid(jnp.dot(z.astype(BF16), wg_ref[...], preferred_element_type=F32) + bg_ref[...])
    return _rms(z * gate, gn_ref[...])


def _ssm_operands(l, ssm, wg, bg, gn):
    layered = (ssm["bre"], ssm["bim"], ssm["cre"], ssm["cimn"], ssm["lre"], ssm["lim"], ssm["d"], wg, bg, gn)
    shared = (ssm["tile_b"], ssm["tile_c"])
    return (layered + shared,
            [_of_layer(l, a.shape[1:]) for a in layered] + [_resident(a.shape) for a in shared])


def _log2(n):
    assert n & (n - 1) == 0
    return n.bit_length() - 1


def _block_diag_scratch():
    return [pltpu.VMEM((N_CHUNK, CH_U, CH_S), BF16), pltpu.VMEM((N_CHUNK, CH_U, CH_S), BF16),
            pltpu.VMEM((N_CHUNK, CH_S, CH_U), BF16), pltpu.VMEM((N_CHUNK, CH_S, CH_U), BF16)]


def _expand_block_diag(t_ref, tile_ref, out_s, rows_per_group, cols_per_group):
    for c in range(N_CHUNK):
        tiled = jnp.dot(t_ref[c], tile_ref[...], preferred_element_type=F32)
        row_group = lax.broadcasted_iota(jnp.int32, tiled.shape, 0) >> _log2(rows_per_group)
        col_group = lax.broadcasted_iota(jnp.int32, tiled.shape, 1) >> _log2(cols_per_group)
        out_s[c] = jnp.where(row_group == col_group, tiled, 0.0).astype(BF16)


def _expand_projections(tb_re, tb_im, tc_re, tc_imn, tile_b, tile_c, bre_s, bim_s, cre_s, cimn_s):
    _expand_block_diag(tb_re, tile_b, bre_s, SSM_GROUP, SSM_STATE)
    _expand_block_diag(tb_im, tile_b, bim_s, SSM_GROUP, SSM_STATE)
    _expand_block_diag(tc_re, tile_c, cre_s, SSM_STATE, SSM_GROUP)
    _expand_block_diag(tc_imn, tile_c, cimn_s, SSM_STATE, SSM_GROUP)


def _ssm_prompt_kernel(u_ref, perm_ref, tb_re, tb_im, tc_re, tc_imn, lre_ref, lim_ref, d_ref,
                       wg_ref, bg_ref, gn_ref, tile_b, tile_c,
                       s_ref, hre_ref, him_ref,
                       bre_ref, bim_ref, cre_ref, cimn_ref,
                       xre_s, xim_s, hbre_s, hbim_s, pre_s, pim_s, cre_s, cim_s, hinre_s, hinim_s, y_s):
    b = pl.program_id(0)
    j = pl.program_id(1)

    @pl.when((b == 0) & (j == 0))
    def _():
        _expand_projections(tb_re, tb_im, tc_re, tc_imn, tile_b, tile_c, bre_ref, bim_ref, cre_ref, cimn_ref)
        lr, li = lre_ref[...], lim_ref[...]
        pr, pi = lr, li
        for i in range(SEG_LEN):
            if i:
                pr, pi = pr * lr - pi * li, pr * li + pi * lr
            pre_s[i * SUBLANES:(i + 1) * SUBLANES, :] = jnp.broadcast_to(pr, (SUBLANES, N_STATE))
            pim_s[i * SUBLANES:(i + 1) * SUBLANES, :] = jnp.broadcast_to(pi, (SUBLANES, N_STATE))

    @pl.when(j == 0)
    def _():
        cre_s[...] = jnp.zeros_like(cre_s)
        cim_s[...] = jnp.zeros_like(cim_s)

    u_nat = jnp.concatenate([u_ref[jb] for jb in range(U_BLOCKS)], axis=1)
    u_perm = jnp.dot(perm_ref[...], u_nat.astype(BF16), preferred_element_type=F32).astype(BF16)

    last = (SEG_LEN - 1) * SUBLANES
    for c in range(N_CHUNK):
        uc = u_perm[:, c * CH_U:(c + 1) * CH_U]
        chunk = slice(c * CH_S, (c + 1) * CH_S)
        xre_s[:, chunk] = jnp.dot(uc, bre_ref[c], preferred_element_type=F32)
        xim_s[:, chunk] = jnp.dot(uc, bim_ref[c], preferred_element_type=F32)

        for lc in range(CH_S // SCAN_LANES):
            sl = slice(c * CH_S + lc * SCAN_LANES, c * CH_S + (lc + 1) * SCAN_LANES)
            lr = jnp.broadcast_to(lre_ref[:, sl], (SUBLANES, SCAN_LANES))
            li = jnp.broadcast_to(lim_ref[:, sl], (SUBLANES, SCAN_LANES))
            hr = xre_s[0:SUBLANES, sl]
            hi = xim_s[0:SUBLANES, sl]
            for i in range(1, SEG_LEN):
                rows = slice(i * SUBLANES, (i + 1) * SUBLANES)
                hr, hi = (lr * hr - li * hi) + xre_s[rows, sl], (lr * hi + li * hr) + xim_s[rows, sl]
                xre_s[rows, sl] = hr
                xim_s[rows, sl] = hi

        l32r = pre_s[last:last + 1, chunk]
        l32i = pim_s[last:last + 1, chunk]
        cr, ci = cre_s[:, chunk], cim_s[:, chunk]
        for seg in range(SUBLANES):
            hinre_s[seg:seg + 1, chunk] = cr
            hinim_s[seg:seg + 1, chunk] = ci
            er = xre_s[last + seg:last + seg + 1, chunk]
            ei = xim_s[last + seg:last + seg + 1, chunk]
            cr, ci = (l32r * cr - l32i * ci) + er, (l32r * ci + l32i * cr) + ei
        cre_s[:, chunk] = cr
        cim_s[:, chunk] = ci

        for lc in range(CH_S // SCAN_LANES):
            sl = slice(c * CH_S + lc * SCAN_LANES, c * CH_S + (lc + 1) * SCAN_LANES)
            hr_in = jnp.concatenate([hinre_s[:, sl]] * 2, axis=0)
            hi_in = jnp.concatenate([hinim_s[:, sl]] * 2, axis=0)
            for i2 in range(SEG_LEN // 2):
                rows = slice(i2 * 2 * SUBLANES, (i2 + 1) * 2 * SUBLANES)
                pr, pi = pre_s[rows, sl], pim_s[rows, sl]
                hbre_s[rows, sl] = (xre_s[rows, sl] + (pr * hr_in - pi * hi_in)).astype(BF16)
                hbim_s[rows, sl] = (xim_s[rows, sl] + (pr * hi_in + pi * hr_in)).astype(BF16)

        y = (jnp.dot(hbre_s[:, chunk], cre_ref[c], preferred_element_type=F32)
             + jnp.dot(hbim_s[:, chunk], cimn_ref[c], preferred_element_type=F32))
        for h in range(BLOCKS_PER_CHUNK):
            y_s[c * BLOCKS_PER_CHUNK + h] = y[:, h * LANES:(h + 1) * LANES]

    y_nat = jnp.concatenate(
        [jnp.concatenate([y_s[jb, pl.ds(i0 * SUBLANES + seg, SUBLANES, stride=SUBLANES), :]
                          for seg in range(SUBLANES) for i0 in range(0, SEG_LEN, SUBLANES)], axis=0)
         for jb in range(U_BLOCKS)], axis=1)
    z = jax.nn.gelu(y_nat + d_ref[...] * u_nat)
    s_ref[...] = _glu_norm(z, wg_ref, bg_ref, gn_ref).astype(BF16)

    @pl.when(j == pl.num_programs(1) - 1)
    def _():
        hre_ref[...] = cre_s[...]
        him_ref[...] = cim_s[...]


def _ssm_prompt(l, u, perm, ssm, wg, bg, gn, batch, seq):
    nt = seq // SSM_TILE
    row = lambda b, j: (b * nt + j, 0)
    st = lambda b, j: (b, 0, 0)
    arrays, specs = _ssm_operands(l, ssm, wg, bg, gn)
    scratch = _block_diag_scratch() + [
        pltpu.VMEM((SSM_TILE, N_STATE), F32),
        pltpu.VMEM((SSM_TILE, N_STATE), F32),
        pltpu.VMEM((SSM_TILE, N_STATE), BF16),
        pltpu.VMEM((SSM_TILE, N_STATE), BF16),
        pltpu.VMEM((SSM_TILE, N_STATE), F32),
        pltpu.VMEM((SSM_TILE, N_STATE), F32),
        pltpu.VMEM((1, N_STATE), F32),
        pltpu.VMEM((1, N_STATE), F32),
        pltpu.VMEM((SUBLANES, N_STATE), F32),
        pltpu.VMEM((SUBLANES, N_STATE), F32),
        pltpu.VMEM((U_BLOCKS, SSM_TILE, LANES), F32),
    ]
    return pl.pallas_call(
        _ssm_prompt_kernel,
        grid=(batch, nt),
        in_specs=[pl.BlockSpec((U_BLOCKS, SSM_TILE, LANES), lambda b, j: (0, b * nt + j, 0)),
                  _resident((SSM_TILE, SSM_TILE))] + specs,
        out_specs=[pl.BlockSpec((SSM_TILE, SSM_WIDTH), row),
                   pl.BlockSpec((None, 1, N_STATE), st),
                   pl.BlockSpec((None, 1, N_STATE), st)],
        out_shape=[jax.ShapeDtypeStruct((batch * seq, SSM_WIDTH), BF16),
                   jax.ShapeDtypeStruct((batch, 1, N_STATE), F32),
                   jax.ShapeDtypeStruct((batch, 1, N_STATE), F32)],
        scratch_shapes=scratch,
        compiler_params=_params(("arbitrary", "arbitrary"), 52),
        name="ssm_prompt",
    )(u, perm, *arrays)


def _ssm_sample_kernel(u_ref, h0re_ref, h0im_ref, tb_re, tb_im, tc_re, tc_imn, lre_ref, lim_ref, d_ref,
                       wg_ref, bg_ref, gn_ref, tile_b, tile_c, s_ref, hre_ref, him_ref,
                       bre_ref, bim_ref, cre_ref, cimn_ref):
    _expand_projections(tb_re, tb_im, tc_re, tc_imn, tile_b, tile_c, bre_ref, bim_ref, cre_ref, cimn_ref)
    u = jnp.concatenate([u_ref[jb] for jb in range(U_BLOCKS)], axis=1)
    zs = []
    for c in range(N_CHUNK):
        cs = slice(c * CH_U, (c + 1) * CH_U)
        ss = slice(c * CH_S, (c + 1) * CH_S)
        uc = u[:, cs].astype(BF16)
        lr, li = lre_ref[:, ss], lim_ref[:, ss]
        h0r, h0i = h0re_ref[:, ss], h0im_ref[:, ss]
        hr = jnp.dot(uc, bre_ref[c], preferred_element_type=F32) + (lr * h0r - li * h0i)
        hi = jnp.dot(uc, bim_ref[c], preferred_element_type=F32) + (lr * h0i + li * h0r)
        hre_ref[:, ss] = hr
        him_ref[:, ss] = hi
        y = (jnp.dot(hr.astype(BF16), cre_ref[c], preferred_element_type=F32)
             + jnp.dot(hi.astype(BF16), cimn_ref[c], preferred_element_type=F32))
        zs.append(jax.nn.gelu(y + d_ref[:, cs] * u[:, cs]))
    z = jnp.concatenate(zs, axis=1)
    s_ref[...] = _glu_norm(z, wg_ref, bg_ref, gn_ref).astype(BF16)


def _ssm_sample(l, u, h0re, h0im, ssm, wg, bg, gn):
    n = u.shape[1]
    arrays, specs = _ssm_operands(l, ssm, wg, bg, gn)
    whole = lambda *s: pl.BlockSpec(s, lambda i: (0,) * len(s))
    return pl.pallas_call(
        _ssm_sample_kernel,
        grid=(1,),
        in_specs=[_resident((U_BLOCKS, n, LANES)), _of_layer(l, (n, N_STATE)), _of_layer(l, (n, N_STATE))] + specs,
        out_specs=[whole(n, SSM_WIDTH), whole(n, N_STATE), whole(n, N_STATE)],
        out_shape=[jax.ShapeDtypeStruct((n, SSM_WIDTH), BF16),
                   jax.ShapeDtypeStruct((n, N_STATE), F32),
                   jax.ShapeDtypeStruct((n, N_STATE), F32)],
        scratch_shapes=_block_diag_scratch(),
        compiler_params=_params(("arbitrary",), 40),
        name="ssm_sample",
    )(u, h0re, h0im, *arrays)


def _out_proj_kernel(x_ref, a_ref, s_ref, w_ref, o_ref):
    acc = (jnp.dot(a_ref[...], w_ref[0:ATTN_WIDTH, :], preferred_element_type=F32)
           + jnp.dot(s_ref[...], w_ref[ATTN_WIDTH:, :], preferred_element_type=F32))
    o_ref[...] = x_ref[...] + acc


def _out_proj(l, x, a, s, w, tm):
    m = x.shape[0]
    row = lambda i: (i, 0)
    return pl.pallas_call(
        _out_proj_kernel,
        grid=(m // tm,),
        in_specs=[pl.BlockSpec((tm, D_MODEL), row),
                  pl.BlockSpec((tm, ATTN_WIDTH), row),
                  pl.BlockSpec((tm, SSM_WIDTH), row),
                  _of_layer(l, (D_MODEL, D_MODEL))],
        out_specs=pl.BlockSpec((tm, D_MODEL), row),
        out_shape=jax.ShapeDtypeStruct((m, D_MODEL), F32),
        compiler_params=_params(("arbitrary",), 48),
        name="out_proj",
    )(x, a, s, w)


FF_TILE = 512
N_FF = D_FF // FF_TILE
FF_SUB = MXU_DIM
CONV_PAD = SUBLANES


def _ffn_conv_specs(l, idx):
    gcol = lambda *g: (l, 0, idx(*g))
    vcol = lambda *g: (l, 0, N_FF + idx(*g))
    return [pl.BlockSpec((None, CONV_W, FF_TILE), gcol), pl.BlockSpec((None, CONV_W, FF_TILE), vcol),
            pl.BlockSpec((None, 1, FF_TILE), gcol), pl.BlockSpec((None, 1, FF_TILE), vcol)]


def _ffn_cast_weight_specs(idx):
    col = lambda *g: (0, idx(*g))
    return [pl.BlockSpec((D_MODEL, FF_TILE), col), pl.BlockSpec((D_MODEL, FF_TILE), col),
            pl.BlockSpec((FF_TILE, D_MODEL), lambda *g: (idx(*g), 0))]


def _ffn_prompt_kernel(final_norm, tiles_per_seq,
                       x_ref, gn_ref, fg_ref, wug_ref, wuv_ref, wd_ref, cwg_ref, cwv_ref, cbg_ref, cbv_ref,
                       o_ref, cg_ref, cv_ref,
                       h_s, extg_s, extv_s, carryg_s, carryv_s):
    m = pl.program_id(0)
    f = pl.program_id(1)
    tm = x_ref.shape[0]

    @pl.when(f == 0)
    def _():
        x = x_ref[...]
        h_s[...] = _rms(x, gn_ref[...]).astype(BF16)
        o_ref[...] = x

    seq_start = (m % tiles_per_seq) == 0

    @pl.when(seq_start)
    def _():
        extg_s[0:CONV_PAD, :] = jnp.zeros((CONV_PAD, FF_TILE), F32)
        extv_s[0:CONV_PAD, :] = jnp.zeros((CONV_PAD, FF_TILE), F32)

    @pl.when(jnp.logical_not(seq_start))
    def _():
        extg_s[0:CONV_PAD, :] = carryg_s[f]
        extv_s[0:CONV_PAD, :] = carryv_s[f]

    subs = [slice(s * FF_SUB, (s + 1) * FF_SUB) for s in range(FF_TILE // FF_SUB)]
    for cs in subs:
        extg_s[CONV_PAD:, cs] = jnp.dot(h_s[...], wug_ref[:, cs], preferred_element_type=F32)
        extv_s[CONV_PAD:, cs] = jnp.dot(h_s[...], wuv_ref[:, cs], preferred_element_type=F32)

    def conv(ext_s, cw_ref, cb_ref, cs):
        out = cb_ref[:, cs]
        for t in range(CONV_W):
            lag = CONV_W - 1 - t
            out = out + cw_ref[t:t + 1, cs] * ext_s[CONV_PAD - lag:CONV_PAD - lag + tm, cs]
        return out

    down = None
    for cs in subs:
        act = (jax.nn.silu(conv(extg_s, cwg_ref, cbg_ref, cs)) * conv(extv_s, cwv_ref, cbv_ref, cs)).astype(BF16)
        part = jnp.dot(act, wd_ref[cs, :], preferred_element_type=F32)
        down = part if down is None else down + part
    o_ref[...] += down

    carryg_s[f] = extg_s[tm:tm + CONV_PAD, :]
    carryv_s[f] = extv_s[tm:tm + CONV_PAD, :]
    cg_ref[f] = extg_s[tm + CONV_PAD - (CONV_W - 1):tm + CONV_PAD, :]
    cv_ref[f] = extv_s[tm + CONV_PAD - (CONV_W - 1):tm + CONV_PAD, :]

    if final_norm:
        @pl.when(f == pl.num_programs(1) - 1)
        def _():
            o_ref[...] = _rms(o_ref[...], fg_ref[...])


def _ffn_prompt(l, x, gn, wug, wuv, wd, cw, cb, fg, batch, seq, tm, final_norm):
    m = batch * seq
    tiles_per_seq = seq // tm
    row = lambda i, f: (i, 0)
    state = lambda i, f: (i // tiles_per_seq, 0, 0, 0)
    return pl.pallas_call(
        functools.partial(_ffn_prompt_kernel, final_norm, tiles_per_seq),
        grid=(m // tm, N_FF),
        in_specs=[pl.BlockSpec((tm, D_MODEL), row, pipeline_mode=pl.Buffered(1)),
                  _of_layer(l, (1, D_MODEL)), _resident((1, D_MODEL))]
                 + _ffn_cast_weight_specs(lambda i, f: f) + _ffn_conv_specs(l, lambda i, f: f),
        out_specs=[pl.BlockSpec((tm, D_MODEL), row),
                   pl.BlockSpec((None, N_FF, CONV_W - 1, FF_TILE), state),
                   pl.BlockSpec((None, N_FF, CONV_W - 1, FF_TILE), state)],
        out_shape=[jax.ShapeDtypeStruct((m, D_MODEL), F32),
                   jax.ShapeDtypeStruct((batch, N_FF, CONV_W - 1, FF_TILE), F32),
                   jax.ShapeDtypeStruct((batch, N_FF, CONV_W - 1, FF_TILE), F32)],
        scratch_shapes=[pltpu.VMEM((tm, D_MODEL), BF16),
                        pltpu.VMEM((tm + CONV_PAD, FF_TILE), F32),
                        pltpu.VMEM((tm + CONV_PAD, FF_TILE), F32),
                        pltpu.VMEM((N_FF, CONV_PAD, FF_TILE), F32),
                        pltpu.VMEM((N_FF, CONV_PAD, FF_TILE), F32)],
        compiler_params=_params(("arbitrary", "arbitrary"), 56),
        name="ffn_prompt",
    )(x, gn, fg, wug, wuv, wd, cw, cw, cb, cb)


CAST_ROWS = 256


def _cast_weight(w_ref, wb_ref):
    for r in range(0, w_ref.shape[0], CAST_ROWS):
        wb_ref[r:r + CAST_ROWS, :] = w_ref[r:r + CAST_ROWS, :].astype(BF16)


def _ffn_sample_kernel(final_norm,
                       x_ref, gn_ref, fg_ref, wug_ref, wuv_ref, wd_ref, cwg_ref, cwv_ref, cbg_ref, cbv_ref,
                       s0g_ref, s0v_ref, s1g_ref, s1v_ref,
                       o_ref, ug_ref, uv_ref, wugb_ref, wuvb_ref, wdb_ref, h_s):
    f = pl.program_id(0)

    @pl.when(f == 0)
    def _():
        x = x_ref[...]
        h_s[...] = _rms(x, gn_ref[...]).astype(BF16)
        o_ref[...] = x

    def conv(w_ref, wb_ref, cw_ref, cb_ref, s0_ref, s1_ref, up_ref):
        _cast_weight(w_ref, wb_ref)
        up = jnp.dot(h_s[...], wb_ref[...], preferred_element_type=F32)
        up_ref[...] = up
        return ((cb_ref[...] + cw_ref[0:1, :] * s0_ref[...]) + cw_ref[1:2, :] * s1_ref[...]) + cw_ref[2:3, :] * up

    gate = conv(wug_ref, wugb_ref, cwg_ref, cbg_ref, s0g_ref, s1g_ref, ug_ref)
    val = conv(wuv_ref, wuvb_ref, cwv_ref, cbv_ref, s0v_ref, s1v_ref, uv_ref)
    act = (jax.nn.silu(gate) * val).astype(BF16)
    _cast_weight(wd_ref, wdb_ref)
    o_ref[...] += jnp.dot(act, wdb_ref[...], preferred_element_type=F32)

    if final_norm:
        @pl.when(f == pl.num_programs(0) - 1)
        def _():
            o_ref[...] = _rms(o_ref[...], fg_ref[...])


def _ffn_sample(l, x, gn, wu, wd, cw, cb, fg, s0, s1, final_norm):
    n = x.shape[0]
    gcol = lambda f: (l, 0, f)
    vcol = lambda f: (l, 0, N_FF + f)
    st = lambda im: pl.BlockSpec((None, n, FF_TILE), im)
    return pl.pallas_call(
        functools.partial(_ffn_sample_kernel, final_norm),
        grid=(N_FF,),
        in_specs=[_resident((n, D_MODEL)), _of_layer(l, (1, D_MODEL)), _resident((1, D_MODEL)),
                  pl.BlockSpec((None, D_MODEL, FF_TILE), gcol), pl.BlockSpec((None, D_MODEL, FF_TILE), vcol),
                  pl.BlockSpec((None, FF_TILE, D_MODEL), lambda f: (l, f, 0))]
                 + _ffn_conv_specs(l, lambda f: f)
                 + [st(gcol), st(vcol), st(gcol), st(vcol)],
        out_specs=[pl.BlockSpec((n, D_MODEL), lambda f: (0, 0)),
                   pl.BlockSpec((n, FF_TILE), lambda f: (0, f)),
                   pl.BlockSpec((n, FF_TILE), lambda f: (0, f))]
                  + _ffn_cast_weight_specs(lambda f: f),
        out_shape=[jax.ShapeDtypeStruct((n, D_MODEL), F32),
                   jax.ShapeDtypeStruct((n, D_FF), F32),
                   jax.ShapeDtypeStruct((n, D_FF), F32),
                   jax.ShapeDtypeStruct((D_MODEL, D_FF), BF16),
                   jax.ShapeDtypeStruct((D_MODEL, D_FF), BF16),
                   jax.ShapeDtypeStruct((D_FF, D_MODEL), BF16)],
        scratch_shapes=[pltpu.VMEM((n, D_MODEL), BF16)],
        compiler_params=_params(("arbitrary",), 44),
        name="ffn_sample",
    )(x, gn, fg, wu, wu, wd, cw, cw, cb, cb, s0, s0, s1, s1)


def _rope_tables(pos):
    half = HEAD_DIM // 2
    inv = ROPE_THETA ** (-jnp.arange(half, dtype=F32) / half)
    ang = pos.astype(F32)[:, None] * inv[None, :]
    cos, sin = jnp.cos(ang), jnp.sin(ang)
    reps = LANES // HEAD_DIM
    return (jnp.concatenate([cos, cos] * reps, axis=1),
            jnp.concatenate([-sin, sin] * reps, axis=1))


def _ssm_params(a_re, a_im, b_re, b_im, c_re, c_im, d, log_dt):
    depth = a_re.shape[0]
    dt = jnp.exp(log_dt)[..., None]
    mag = jnp.exp(a_re * dt)
    lr, li = mag * jnp.cos(a_im * dt), mag * jnp.sin(a_im * dt)
    nr, ni = lr - 1.0, li
    den = a_re * a_re + a_im * a_im
    qr = (nr * a_re + ni * a_im) / den
    qi = (ni * a_re - nr * a_im) / den
    bbr = qr[..., None] * b_re - qi[..., None] * b_im
    bbi = qr[..., None] * b_im + qi[..., None] * b_re

    def compact(t):
        r, c = t.shape[2], t.shape[3]
        t = t.reshape(depth, N_CHUNK, GROUP_CHUNK * r, c).astype(BF16)
        return jnp.pad(t, ((0, 0), (0, 0), (0, 0), (0, LANES - c)))

    def lane_tiling(c):
        src = jnp.arange(LANES)[:, None]
        dst = jnp.arange(GROUP_CHUNK * c)[None, :] % c
        return (src == dst).astype(BF16)

    return dict(bre=compact(bbr.transpose(0, 1, 3, 2)), bim=compact(bbi.transpose(0, 1, 3, 2)),
                cre=compact(c_re.transpose(0, 1, 3, 2)), cimn=compact((-c_im).transpose(0, 1, 3, 2)),
                tile_b=lane_tiling(SSM_STATE), tile_c=lane_tiling(SSM_GROUP),
                lre=lr.reshape(depth, 1, N_STATE), lim=li.reshape(depth, 1, N_STATE),
                d=d.reshape(depth, 1, SSM_WIDTH))


def _segment_permutation():
    r = jnp.arange(SSM_TILE)
    src = (r % SUBLANES) * SEG_LEN + r // SUBLANES
    return (src[:, None] == jnp.arange(SSM_TILE)[None, :]).astype(BF16)


def kernel(x_prompt, x_sample, cache_k, cache_v, state_ssm_re, state_ssm_im, state_conv, attn_norm_g, w_in, attn_sinks, ssm_a_re, ssm_a_im, ssm_b_re, ssm_b_im, ssm_c_re, ssm_c_im, ssm_d, ssm_log_dt, w_glu, b_glu, attn_out_norm_g, ssm_out_norm_g, w_out, ffn_norm_g, w_up, conv_w, conv_b, w_down, final_norm_g):
    batch, seq, _ = x_prompt.shape
    nsamp, dec_seq, _ = x_sample.shape
    wbuf = cache_k.shape[2]
    assert dec_seq == 1 and wbuf == WINDOW and seq % SSM_TILE == 0
    assert PAST_LEN >= wbuf

    w_in_b, w_glu_b, w_out_b = w_in.astype(BF16), w_glu.astype(BF16), w_out.astype(BF16)
    ssm = _ssm_params(ssm_a_re, ssm_a_im, ssm_b_re, ssm_b_im, ssm_c_re, ssm_c_im, ssm_d, ssm_log_dt)
    perm = _segment_permutation()

    cos_p, sin_p = _rope_tables(jnp.arange(seq, dtype=jnp.int32))
    cos_s, sin_s = _rope_tables(jnp.full((nsamp,), PAST_LEN, dtype=jnp.int32))

    rows = lambda a: a.reshape(a.shape[0], 1, a.shape[-1])
    g_in, g_a, g_s, g_f = rows(attn_norm_g), rows(attn_out_norm_g), rows(ssm_out_norm_g), rows(ffn_norm_g)
    bg, cb = rows(b_glu), rows(conv_b)
    fg = final_norm_g.reshape(1, D_MODEL)
    sink_col = attn_sinks.reshape(DEPTH, N_HEADS, 1)
    head_is_lo = (jnp.arange(N_HEADS) < KV_REP)[None, :, None]
    kc = cache_k.reshape(DEPTH, nsamp, wbuf, KV_COLS)
    vc = cache_v.reshape(DEPTH, nsamp, wbuf, KV_COLS)
    h0re = state_ssm_re.reshape(DEPTH, nsamp, N_STATE)
    h0im = state_ssm_im.reshape(DEPTH, nsamp, N_STATE)
    conv0, conv1 = state_conv[:, :, 0, :], state_conv[:, :, 1, :]

    xp = x_prompt.reshape(batch * seq, D_MODEL)
    xs = x_sample.reshape(nsamp, D_MODEL)
    tm_p = 512
    tm_ffn = 1024
    outs = {k: [] for k in ("kp", "vp", "hrp", "hip", "cgp", "cvp", "ks", "vs", "hrs", "his", "ugs", "uvs")}

    for l in range(DEPTH):
        last = l == DEPTH - 1

        q, k, v, u = _in_proj(l, xs, g_in, w_in_b, cos_s, sin_s, nsamp)
        qh = q.reshape(nsamp, N_HEADS, HEAD_DIM)
        zq = jnp.zeros_like(qh)
        qbd = jnp.where(head_is_lo, jnp.concatenate([qh, zq], axis=-1), jnp.concatenate([zq, qh], axis=-1))
        o = _attn_sample(l, qbd, kc, vc, k.reshape(nsamp, 1, KV_COLS), v.reshape(nsamp, 1, KV_COLS), sink_col, 8)
        a_raw = jnp.concatenate([o[:, :KV_REP, :HEAD_DIM].reshape(nsamp, -1),
                                 o[:, KV_REP:, HEAD_DIM:].reshape(nsamp, -1)], axis=1)
        a = _rms_cast(l, a_raw, g_a)
        s, hre, him = _ssm_sample(l, u, h0re, h0im, ssm, w_glu_b, bg, g_s)
        x1 = _out_proj(l, xs, a, s, w_out_b, nsamp)
        xs, ug, uv, wug_b, wuv_b, wd_b = _ffn_sample(l, x1, g_f, w_up, w_down, conv_w, cb, fg, conv0, conv1, last)
        outs["ks"].append(k)
        outs["vs"].append(v)
        outs["hrs"].append(hre)
        outs["his"].append(him)
        outs["ugs"].append(ug)
        outs["uvs"].append(uv)

        q, k, v, u = _in_proj(l, xp, g_in, w_in_b, cos_p, sin_p, tm_p)
        a = _attn_prompt(l, q, k, v, attn_sinks, g_a, batch, seq)
        s, hre, him = _ssm_prompt(l, u, perm, ssm, w_glu_b, bg, g_s, batch, seq)
        x1 = _out_proj(l, xp, a, s, w_out_b, tm_p)
        xp, cg, cv = _ffn_prompt(l, x1, g_f, wug_b, wuv_b, wd_b, conv_w, cb, fg, batch, seq, tm_ffn, last)
        outs["kp"].append(k.reshape(batch, seq, KV_COLS)[:, seq - WINDOW:])
        outs["vp"].append(v.reshape(batch, seq, KV_COLS)[:, seq - WINDOW:])
        outs["hrp"].append(hre)
        outs["hip"].append(him)
        outs["cgp"].append(cg)
        outs["cvp"].append(cv)

    st = lambda name: jnp.stack(outs[name], axis=0)
    heads = lambda t: t.reshape(t.shape[:-1] + (N_KV_HEADS, HEAD_DIM))
    states = lambda t: t.reshape(DEPTH, -1, SSM_GROUPS, SSM_STATE)
    unblock = lambda t: t.transpose(0, 1, 3, 2, 4).reshape(DEPTH, batch, CONV_W - 1, D_FF)
    conv_prompt = jnp.concatenate([unblock(st("cgp")), unblock(st("cvp"))], axis=-1)
    k_sample = jnp.concatenate([cache_k[:, :, 1:], heads(st("ks"))[:, :, None]], axis=2)
    v_sample = jnp.concatenate([cache_v[:, :, 1:], heads(st("vs"))[:, :, None]], axis=2)
    conv_sample = jnp.stack([conv1, jnp.concatenate([st("ugs"), st("uvs")], axis=-1)], axis=2)
    return (xp.reshape(batch, seq, D_MODEL), xs.reshape(nsamp, 1, D_MODEL),
            heads(st("kp")), heads(st("vp")), states(st("hrp")), states(st("hip")), conv_prompt,
            k_sample, v_sample, states(st("hrs")), states(st("his")), conv_sample)
```

```python
import functools

import jax
import jax.numpy as jnp
from jax import lax
from jax.experimental import pallas as pl
from jax.experimental.pallas import tpu as pltpu

F32 = jnp.float32
BF16 = jnp.bfloat16

D_MODEL = 2048
DEPTH = 4
ATTN_WIDTH = 1024
SSM_WIDTH = 1024
HEAD_DIM = 64
N_HEADS = 16
N_KV_HEADS = 2
KV_REP = 8
KV_COLS = N_KV_HEADS * HEAD_DIM
WINDOW = 128
ROPE_THETA = 10000.0
SSM_GROUP = 16
SSM_GROUPS = 64
SSM_STATE = 64
N_STATE = SSM_GROUPS * SSM_STATE
D_FF = 5632
CONV_W = 3
RMS_EPS = 1e-6
IN_COLS = ATTN_WIDTH + 2 * KV_COLS + SSM_WIDTH
NEG_INF = -1e30
PAST_LEN = 16384
SCORE_SCALE = HEAD_DIM ** -0.5
assert SCORE_SCALE == 2.0 ** -3

LANES = 128
SUBLANES = 8
MXU_DIM = 256
MIB = 1024 * 1024
U_BLOCKS = SSM_WIDTH // LANES

GROUP_CHUNK = 16
N_CHUNK = SSM_GROUPS // GROUP_CHUNK
CH_U = GROUP_CHUNK * SSM_GROUP
CH_S = GROUP_CHUNK * SSM_STATE
BLOCKS_PER_CHUNK = CH_U // LANES

SEG_LEN = 32
SSM_TILE = SUBLANES * SEG_LEN
SCAN_LANES = 512


def _rms(x, g):
    ms = jnp.mean(x * x, axis=-1, keepdims=True)
    return x * lax.rsqrt(ms + RMS_EPS) * g


def _params(sem, vmem_mib):
    return pltpu.CompilerParams(dimension_semantics=sem, vmem_limit_bytes=vmem_mib * MIB)


def _resident(shape):
    nd = len(shape)
    return pl.BlockSpec(shape, lambda *_: (0,) * nd, pipeline_mode=pl.Buffered(1))


def _of_layer(l, shape):
    nd = len(shape)
    return pl.BlockSpec((None,) + tuple(shape), lambda *_: (l,) + (0,) * nd, pipeline_mode=pl.Buffered(1))


def _in_proj_kernel(x_ref, g_ref, w_ref, cos_ref, sin_ref, q_ref, k_ref, v_ref, u_ref):
    h = _rms(x_ref[...], g_ref[...]).astype(BF16)
    proj = jnp.dot(h, w_ref[...], preferred_element_type=F32)
    cos = cos_ref[...]
    sin = sin_ref[...]
    lane = lax.broadcasted_iota(jnp.int32, cos.shape, 1)
    first_half = (lane & (HEAD_DIM - 1)) < (HEAD_DIM // 2)

    def rope(blk):
        partner = jnp.where(first_half,
                            pltpu.roll(blk, LANES - HEAD_DIM // 2, 1),
                            pltpu.roll(blk, HEAD_DIM // 2, 1))
        return blk * cos + partner * sin

    for j in range(ATTN_WIDTH // LANES):
        q_ref[:, j * LANES:(j + 1) * LANES] = (rope(proj[:, j * LANES:(j + 1) * LANES]) * SCORE_SCALE).astype(BF16)
    k_ref[...] = rope(proj[:, ATTN_WIDTH:ATTN_WIDTH + KV_COLS])
    v_ref[...] = proj[:, ATTN_WIDTH + KV_COLS:ATTN_WIDTH + 2 * KV_COLS]
    u0 = ATTN_WIDTH + 2 * KV_COLS
    for j in range(U_BLOCKS):
        u_ref[j] = proj[:, u0 + j * LANES:u0 + (j + 1) * LANES]


def _in_proj(l, x, g, w, cos, sin, tm):
    m = x.shape[0]
    pos_tiles = cos.shape[0] // tm
    row = lambda i: (i, 0)
    return pl.pallas_call(
        _in_proj_kernel,
        grid=(m // tm,),
        in_specs=[pl.BlockSpec((tm, D_MODEL), row),
                  _of_layer(l, (1, D_MODEL)),
                  _of_layer(l, (D_MODEL, IN_COLS)),
                  pl.BlockSpec((tm, LANES), lambda i: (i % pos_tiles, 0)),
                  pl.BlockSpec((tm, LANES), lambda i: (i % pos_tiles, 0))],
        out_specs=[pl.BlockSpec((tm, ATTN_WIDTH), row),
                   pl.BlockSpec((tm, KV_COLS), row),
                   pl.BlockSpec((tm, KV_COLS), row),
                   pl.BlockSpec((U_BLOCKS, tm, LANES), lambda i: (0, i, 0))],
        out_shape=[jax.ShapeDtypeStruct((m, ATTN_WIDTH), BF16),
                   jax.ShapeDtypeStruct((m, KV_COLS), F32),
                   jax.ShapeDtypeStruct((m, KV_COLS), F32),
                   jax.ShapeDtypeStruct((U_BLOCKS, m, LANES), F32)],
        compiler_params=_params(("arbitrary",), 48),
        name="in_proj",
    )(x, g, w, cos, sin)


ATTN_SUB = 4


def _attn_prompt_kernel(layer, sinks_ref, q_ref, kc_ref, kp_ref, vc_ref, vp_ref, gn_ref, a_ref):
    j = pl.program_id(1)
    lane = lax.broadcasted_iota(jnp.int32, (WINDOW, LANES), 1)
    lo = lane < HEAD_DIM
    qi = lax.broadcasted_iota(jnp.int32, (WINDOW, 2 * WINDOW), 0)
    kj = lax.broadcasted_iota(jnp.int32, (WINDOW, 2 * WINDOW), 1)
    diff = qi + WINDOW - kj
    band = (diff >= 0) & (diff <= WINDOW)
    first_key = jnp.where(j > 0, 0, WINDOW)
    valid = [band & (kj >= first_key)] + [band] * (ATTN_SUB - 1)
    zero = jnp.zeros((WINDOW, LANES), F32)

    def halves(x, g):
        r = pltpu.roll(x, HEAD_DIM, 1)
        if g == 0:
            return jnp.where(lo, x, zero).astype(BF16), jnp.where(lo, zero, r).astype(BF16)
        return jnp.where(lo, r, zero).astype(BF16), jnp.where(lo, zero, x).astype(BF16)

    rows = lambda sb: slice(sb * WINDOW, (sb + 1) * WINDOW)
    k_blocks = [kp_ref[...]] + [kc_ref[rows(sb), :] for sb in range(ATTN_SUB)]
    v_blocks = [vp_ref[...]] + [vc_ref[rows(sb), :] for sb in range(ATTN_SUB)]
    k_halves = [[halves(x, g) for x in k_blocks] for g in range(N_KV_HEADS)]
    v_halves = [[halves(x, g) for x in v_blocks] for g in range(N_KV_HEADS)]
    pairs = KV_REP // 2
    nt = (((1,), (1,)), ((), ()))

    for sb in range(ATTN_SUB):
        def softmax(s, head, sb=sb):
            s = jnp.where(valid[sb], s, NEG_INF)
            sink = sinks_ref[layer, head]
            m = jnp.maximum(jnp.max(s, axis=-1, keepdims=True), sink)
            p = jnp.exp(s - m)
            denom = jnp.sum(p, axis=-1, keepdims=True) + jnp.exp(sink - m)
            return (p / denom).astype(BF16)

        out_blocks = []
        for g in range(N_KV_HEADS):
            window = lambda hv, half: jnp.concatenate([hv[g][sb][half], hv[g][sb + 1][half]], axis=0)
            k_lo, k_hi, v_lo, v_hi = window(k_halves, 0), window(k_halves, 1), window(v_halves, 0), window(v_halves, 1)
            qg = jnp.concatenate([q_ref[rows(sb), (g * pairs + p) * LANES:(g * pairs + p + 1) * LANES]
                                  for p in range(pairs)], axis=0)
            s_even = lax.dot_general(qg, k_lo, nt, preferred_element_type=F32)
            s_odd = lax.dot_general(qg, k_hi, nt, preferred_element_type=F32)
            p_even = jnp.concatenate(
                [softmax(s_even[rows(p)], g * KV_REP + 2 * p) for p in range(pairs)], axis=0)
            p_odd = jnp.concatenate(
                [softmax(s_odd[rows(p)], g * KV_REP + 2 * p + 1) for p in range(pairs)], axis=0)
            o = (jnp.dot(p_even, v_lo, preferred_element_type=F32)
                 + jnp.dot(p_odd, v_hi, preferred_element_type=F32))
            out_blocks += [o[rows(p)] for p in range(pairs)]

        ssq = out_blocks[0] * out_blocks[0]
        for blk in out_blocks[1:]:
            ssq = ssq + blk * blk
        inv = lax.rsqrt(jnp.sum(ssq, axis=-1, keepdims=True) / ATTN_WIDTH + RMS_EPS)
        for i, blk in enumerate(out_blocks):
            cols = slice(i * LANES, (i + 1) * LANES)
            a_ref[rows(sb), cols] = (blk * inv * gn_ref[:, cols]).astype(BF16)


def _attn_prompt(l, q, k, v, sinks, gn, batch, seq):
    nb = seq // WINDOW
    steps = nb // ATTN_SUB
    cur = lambda b, j: (b * steps + j, 0)
    prev = lambda b, j: (b * nb + jnp.maximum(j * ATTN_SUB - 1, 0), 0)
    return pl.pallas_call(
        functools.partial(_attn_prompt_kernel, l),
        grid=(batch, steps),
        in_specs=[pl.BlockSpec(memory_space=pltpu.SMEM),
                  pl.BlockSpec((ATTN_SUB * WINDOW, ATTN_WIDTH), cur),
                  pl.BlockSpec((ATTN_SUB * WINDOW, KV_COLS), cur), pl.BlockSpec((WINDOW, KV_COLS), prev),
                  pl.BlockSpec((ATTN_SUB * WINDOW, KV_COLS), cur), pl.BlockSpec((WINDOW, KV_COLS), prev),
                  _of_layer(l, (1, ATTN_WIDTH))],
        out_specs=pl.BlockSpec((ATTN_SUB * WINDOW, ATTN_WIDTH), cur),
        out_shape=jax.ShapeDtypeStruct((batch * seq, ATTN_WIDTH), BF16),
        compiler_params=_params(("arbitrary", "arbitrary"), 32),
        name="attn_prompt",
    )(sinks, q, k, k, v, v, gn)


def _attn_sample_kernel(qbd_ref, kc_ref, vc_ref, kn_ref, vn_ref, sink_ref, o_ref):
    bb = qbd_ref.shape[0]
    rows, keys = bb * N_HEADS, bb * WINDOW
    q = qbd_ref[...].reshape(rows, LANES)
    kc = kc_ref[...].reshape(keys, KV_COLS).astype(BF16)
    vc = vc_ref[...].reshape(keys, KV_COLS).astype(BF16)
    per_head = lambda ref: jnp.broadcast_to(ref[...], (bb, N_HEADS, KV_COLS)).reshape(rows, KV_COLS)
    kn, vn = per_head(kn_ref), per_head(vn_ref)
    sink = jnp.concatenate([sink_ref[...]] * bb, axis=0)
    nt = (((1,), (1,)), ((), ()))
    s = lax.dot_general(q, kc, nt, preferred_element_type=F32)
    own = ((lax.broadcasted_iota(jnp.int32, s.shape, 0) >> _log2(N_HEADS))
           == (lax.broadcasted_iota(jnp.int32, s.shape, 1) >> _log2(WINDOW)))
    s = jnp.where(own, s, NEG_INF)
    s_new = jnp.sum(q.astype(F32) * kn, axis=-1, keepdims=True)
    m = jnp.maximum(jnp.maximum(jnp.max(s, axis=-1, keepdims=True), s_new), sink)
    p = jnp.exp(s - m)
    p_new = jnp.exp(s_new - m)
    denom = jnp.sum(p, axis=-1, keepdims=True) + p_new + jnp.exp(sink - m)
    o = jnp.dot((p / denom).astype(BF16), vc, preferred_element_type=F32) + (p_new / denom) * vn
    o_ref[...] = o.reshape(bb, N_HEADS, LANES)


def _attn_sample(l, qbd, kc, vc, kn, vn, sink_col, bb):
    nbatch = qbd.shape[0]
    blk = lambda *s: pl.BlockSpec((bb,) + s, lambda i: (i, 0, 0))
    cache = pl.BlockSpec((None, bb, WINDOW, KV_COLS), lambda i: (l, i, 0, 0))
    return pl.pallas_call(
        _attn_sample_kernel,
        grid=(nbatch // bb,),
        in_specs=[blk(N_HEADS, LANES), cache, cache,
                  blk(1, KV_COLS), blk(1, KV_COLS), _of_layer(l, (N_HEADS, 1))],
        out_specs=blk(N_HEADS, LANES),
        out_shape=jax.ShapeDtypeStruct((nbatch, N_HEADS, LANES), F32),
        compiler_params=_params(("arbitrary",), 32),
        name="attn_sample",
    )(qbd, kc, vc, kn, vn, sink_col)


def _rms_cast_kernel(x_ref, g_ref, o_ref):
    o_ref[...] = _rms(x_ref[...], g_ref[...]).astype(BF16)


def _rms_cast(l, x, g):
    n, d = x.shape
    return pl.pallas_call(
        _rms_cast_kernel,
        grid=(1,),
        in_specs=[_resident((n, d)), _of_layer(l, (1, d))],
        out_specs=pl.BlockSpec((n, d), lambda i: (0, 0)),
        out_shape=jax.ShapeDtypeStruct(x.shape, BF16),
        name="rms_cast",
    )(x, g)


def _glu_norm(z, wg_ref, bg_ref, gn_ref):
    gate = jax.nn.sigmoid(jnp.dot(z.astype(BF16), wg_ref[...], preferred_element_type=F32) + bg_ref[...])
    return _rms(z * gate, gn_ref[...])


def _ssm_operands(l, ssm, wg, bg, gn):
    layered = (ssm["bre"], ssm["bim"], ssm["cre"], ssm["cimn"], ssm["lre"], ssm["lim"], ssm["d"], wg, bg, gn)
    shared = (ssm["tile_b"], ssm["tile_c"])
    return (layered + shared,
            [_of_layer(l, a.shape[1:]) for a in layered] + [_resident(a.shape) for a in shared])


def _log2(n):
    assert n & (n - 1) == 0
    return n.bit_length() - 1


def _block_diag_scratch():
    return [pltpu.VMEM((N_CHUNK, CH_U, CH_S), BF16), pltpu.VMEM((N_CHUNK, CH_U, CH_S), BF16),
            pltpu.VMEM((N_CHUNK, CH_S, CH_U), BF16), pltpu.VMEM((N_CHUNK, CH_S, CH_U), BF16)]


def _expand_block_diag(t_ref, tile_ref, out_s, rows_per_group, cols_per_group):
    for c in range(N_CHUNK):
        tiled = jnp.dot(t_ref[c], tile_ref[...], preferred_element_type=F32)
        row_group = lax.broadcasted_iota(jnp.int32, tiled.shape, 0) >> _log2(rows_per_group)
        col_group = lax.broadcasted_iota(jnp.int32, tiled.shape, 1) >> _log2(cols_per_group)
        out_s[c] = jnp.where(row_group == col_group, tiled, 0.0).astype(BF16)


def _expand_projections(tb_re, tb_im, tc_re, tc_imn, tile_b, tile_c, bre_s, bim_s, cre_s, cimn_s):
    _expand_block_diag(tb_re, tile_b, bre_s, SSM_GROUP, SSM_STATE)
    _expand_block_diag(tb_im, tile_b, bim_s, SSM_GROUP, SSM_STATE)
    _expand_block_diag(tc_re, tile_c, cre_s, SSM_STATE, SSM_GROUP)
    _expand_block_diag(tc_imn, tile_c, cimn_s, SSM_STATE, SSM_GROUP)


def _ssm_prompt_kernel(u_ref, perm_ref, tb_re, tb_im, tc_re, tc_imn, lre_ref, lim_ref, d_ref,
                       wg_ref, bg_ref, gn_ref, tile_b, tile_c,
                       s_ref, hre_ref, him_ref,
                       bre_ref, bim_ref, cre_ref, cimn_ref,
                       xre_s, xim_s, hbre_s, hbim_s, pre_s, pim_s, cre_s, cim_s, hinre_s, hinim_s, y_s):
    b = pl.program_id(0)
    j = pl.program_id(1)

    @pl.when((b == 0) & (j == 0))
    def _():
        _expand_projections(tb_re, tb_im, tc_re, tc_imn, tile_b, tile_c, bre_ref, bim_ref, cre_ref, cimn_ref)
        lr, li = lre_ref[...], lim_ref[...]
        pr, pi = lr, li
        for i in range(SEG_LEN):
            if i:
                pr, pi = pr * lr - pi * li, pr * li + pi * lr
            pre_s[i * SUBLANES:(i + 1) * SUBLANES, :] = jnp.broadcast_to(pr, (SUBLANES, N_STATE))
            pim_s[i * SUBLANES:(i + 1) * SUBLANES, :] = jnp.broadcast_to(pi, (SUBLANES, N_STATE))

    @pl.when(j == 0)
    def _():
        cre_s[...] = jnp.zeros_like(cre_s)
        cim_s[...] = jnp.zeros_like(cim_s)

    u_nat = jnp.concatenate([u_ref[jb] for jb in range(U_BLOCKS)], axis=1)
    u_perm = jnp.dot(perm_ref[...], u_nat.astype(BF16), preferred_element_type=F32).astype(BF16)

    last = (SEG_LEN - 1) * SUBLANES
    for c in range(N_CHUNK):
        uc = u_perm[:, c * CH_U:(c + 1) * CH_U]
        chunk = slice(c * CH_S, (c + 1) * CH_S)
        xre_s[:, chunk] = jnp.dot(uc, bre_ref[c], preferred_element_type=F32)
        xim_s[:, chunk] = jnp.dot(uc, bim_ref[c], preferred_element_type=F32)

        for lc in range(CH_S // SCAN_LANES):
            sl = slice(c * CH_S + lc * SCAN_LANES, c * CH_S + (lc + 1) * SCAN_LANES)
            lr = jnp.broadcast_to(lre_ref[:, sl], (SUBLANES, SCAN_LANES))
            li = jnp.broadcast_to(lim_ref[:, sl], (SUBLANES, SCAN_LANES))
            hr = xre_s[0:SUBLANES, sl]
            hi = xim_s[0:SUBLANES, sl]
            for i in range(1, SEG_LEN):
                rows = slice(i * SUBLANES, (i + 1) * SUBLANES)
                hr, hi = (lr * hr - li * hi) + xre_s[rows, sl], (lr * hi + li * hr) + xim_s[rows, sl]
                xre_s[rows, sl] = hr
                xim_s[rows, sl] = hi

        l32r = pre_s[last:last + 1, chunk]
        l32i = pim_s[last:last + 1, chunk]
        cr, ci = cre_s[:, chunk], cim_s[:, chunk]
        for seg in range(SUBLANES):
            hinre_s[seg:seg + 1, chunk] = cr
            hinim_s[seg:seg + 1, chunk] = ci
            er = xre_s[last + seg:last + seg + 1, chunk]
            ei = xim_s[last + seg:last + seg + 1, chunk]
            cr, ci = (l32r * cr - l32i * ci) + er, (l32r * ci + l32i * cr) + ei
        cre_s[:, chunk] = cr
        cim_s[:, chunk] = ci

        for lc in range(CH_S // SCAN_LANES):
            sl = slice(c * CH_S + lc * SCAN_LANES, c * CH_S + (lc + 1) * SCAN_LANES)
            hr_in = jnp.concatenate([hinre_s[:, sl]] * 2, axis=0)
            hi_in = jnp.concatenate([hinim_s[:, sl]] * 2, axis=0)
            for i2 in range(SEG_LEN // 2):
                rows = slice(i2 * 2 * SUBLANES, (i2 + 1) * 2 * SUBLANES)
                pr, pi = pre_s[rows, sl], pim_s[rows, sl]
                hbre_s[rows, sl] = (xre_s[rows, sl] + (pr * hr_in - pi * hi_in)).astype(BF16)
                hbim_s[rows, sl] = (xim_s[rows, sl] + (pr * hi_in + pi * hr_in)).astype(BF16)

        y = (jnp.dot(hbre_s[:, chunk], cre_ref[c], preferred_element_type=F32)
             + jnp.dot(hbim_s[:, chunk], cimn_ref[c], preferred_element_type=F32))
        for h in range(BLOCKS_PER_CHUNK):
            y_s[c * BLOCKS_PER_CHUNK + h] = y[:, h * LANES:(h + 1) * LANES]

    y_nat = jnp.concatenate(
        [jnp.concatenate([y_s[jb, pl.ds(i0 * SUBLANES + seg, SUBLANES, stride=SUBLANES), :]
                          for seg in range(SUBLANES) for i0 in range(0, SEG_LEN, SUBLANES)], axis=0)
         for jb in range(U_BLOCKS)], axis=1)
    z = jax.nn.gelu(y_nat + d_ref[...] * u_nat)
    s_ref[...] = _glu_norm(z, wg_ref, bg_ref, gn_ref).astype(BF16)

    @pl.when(j == pl.num_programs(1) - 1)
    def _():
        hre_ref[...] = cre_s[...]
        him_ref[...] = cim_s[...]


def _ssm_prompt(l, u, perm, ssm, wg, bg, gn, batch, seq):
    nt = seq // SSM_TILE
    row = lambda b, j: (b * nt + j, 0)
    st = lambda b, j: (b, 0, 0)
    arrays, specs = _ssm_operands(l, ssm, wg, bg, gn)
    scratch = _block_diag_scratch() + [
        pltpu.VMEM((SSM_TILE, N_STATE), F32),
        pltpu.VMEM((SSM_TILE, N_STATE), F32),
        pltpu.VMEM((SSM_TILE, N_STATE), BF16),
        pltpu.VMEM((SSM_TILE, N_STATE), BF16),
        pltpu.VMEM((SSM_TILE, N_STATE), F32),
        pltpu.VMEM((SSM_TILE, N_STATE), F32),
        pltpu.VMEM((1, N_STATE), F32),
        pltpu.VMEM((1, N_STATE), F32),
        pltpu.VMEM((SUBLANES, N_STATE), F32),
        pltpu.VMEM((SUBLANES, N_STATE), F32),
        pltpu.VMEM((U_BLOCKS, SSM_TILE, LANES), F32),
    ]
    return pl.pallas_call(
        _ssm_prompt_kernel,
        grid=(batch, nt),
        in_specs=[pl.BlockSpec((U_BLOCKS, SSM_TILE, LANES), lambda b, j: (0, b * nt + j, 0)),
                  _resident((SSM_TILE, SSM_TILE))] + specs,
        out_specs=[pl.BlockSpec((SSM_TILE, SSM_WIDTH), row),
                   pl.BlockSpec((None, 1, N_STATE), st),
                   pl.BlockSpec((None, 1, N_STATE), st)],
        out_shape=[jax.ShapeDtypeStruct((batch * seq, SSM_WIDTH), BF16),
                   jax.ShapeDtypeStruct((batch, 1, N_STATE), F32),
                   jax.ShapeDtypeStruct((batch, 1, N_STATE), F32)],
        scratch_shapes=scratch,
        compiler_params=_params(("arbitrary", "arbitrary"), 52),
        name="ssm_prompt",
    )(u, perm, *arrays)


def _ssm_sample_kernel(u_ref, h0re_ref, h0im_ref, tb_re, tb_im, tc_re, tc_imn, lre_ref, lim_ref, d_ref,
                       wg_ref, bg_ref, gn_ref, tile_b, tile_c, s_ref, hre_ref, him_ref,
                       bre_ref, bim_ref, cre_ref, cimn_ref):
    _expand_projections(tb_re, tb_im, tc_re, tc_imn, tile_b, tile_c, bre_ref, bim_ref, cre_ref, cimn_ref)
    u = jnp.concatenate([u_ref[jb] for jb in range(U_BLOCKS)], axis=1)
    zs = []
    for c in range(N_CHUNK):
        cs = slice(c * CH_U, (c + 1) * CH_U)
        ss = slice(c * CH_S, (c + 1) * CH_S)
        uc = u[:, cs].astype(BF16)
        lr, li = lre_ref[:, ss], lim_ref[:, ss]
        h0r, h0i = h0re_ref[:, ss], h0im_ref[:, ss]
        hr = jnp.dot(uc, bre_ref[c], preferred_element_type=F32) + (lr * h0r - li * h0i)
        hi = jnp.dot(uc, bim_ref[c], preferred_element_type=F32) + (lr * h0i + li * h0r)
        hre_ref[:, ss] = hr
        him_ref[:, ss] = hi
        y = (jnp.dot(hr.astype(BF16), cre_ref[c], preferred_element_type=F32)
             + jnp.dot(hi.astype(BF16), cimn_ref[c], preferred_element_type=F32))
        zs.append(jax.nn.gelu(y + d_ref[:, cs] * u[:, cs]))
    z = jnp.concatenate(zs, axis=1)
    s_ref[...] = _glu_norm(z, wg_ref, bg_ref, gn_ref).astype(BF16)


def _ssm_sample(l, u, h0re, h0im, ssm, wg, bg, gn):
    n = u.shape[1]
    arrays, specs = _ssm_operands(l, ssm, wg, bg, gn)
    whole = lambda *s: pl.BlockSpec(s, lambda i: (0,) * len(s))
    return pl.pallas_call(
        _ssm_sample_kernel,
        grid=(1,),
        in_specs=[_resident((U_BLOCKS, n, LANES)), _of_layer(l, (n, N_STATE)), _of_layer(l, (n, N_STATE))] + specs,
        out_specs=[whole(n, SSM_WIDTH), whole(n, N_STATE), whole(n, N_STATE)],
        out_shape=[jax.ShapeDtypeStruct((n, SSM_WIDTH), BF16),
                   jax.ShapeDtypeStruct((n, N_STATE), F32),
                   jax.ShapeDtypeStruct((n, N_STATE), F32)],
        scratch_shapes=_block_diag_scratch(),
        compiler_params=_params(("arbitrary",), 40),
        name="ssm_sample",
    )(u, h0re, h0im, *arrays)


def _out_proj_kernel(x_ref, a_ref, s_ref, w_ref, o_ref):
    acc = (jnp.dot(a_ref[...], w_ref[0:ATTN_WIDTH, :], preferred_element_type=F32)
           + jnp.dot(s_ref[...], w_ref[ATTN_WIDTH:, :], preferred_element_type=F32))
    o_ref[...] = x_ref[...] + acc


def _out_proj(l, x, a, s, w, tm):
    m = x.shape[0]
    row = lambda i: (i, 0)
    return pl.pallas_call(
        _out_proj_kernel,
        grid=(m // tm,),
        in_specs=[pl.BlockSpec((tm, D_MODEL), row),
                  pl.BlockSpec((tm, ATTN_WIDTH), row),
                  pl.BlockSpec((tm, SSM_WIDTH), row),
                  _of_layer(l, (D_MODEL, D_MODEL))],
        out_specs=pl.BlockSpec((tm, D_MODEL), row),
        out_shape=jax.ShapeDtypeStruct((m, D_MODEL), F32),
        compiler_params=_params(("arbitrary",), 48),
        name="out_proj",
    )(x, a, s, w)


FF_TILE = 512
N_FF = D_FF // FF_TILE
FF_SUB = MXU_DIM
CONV_PAD = SUBLANES


def _ffn_conv_specs(l, idx):
    gcol = lambda *g: (l, 0, idx(*g))
    vcol = lambda *g: (l, 0, N_FF + idx(*g))
    return [pl.BlockSpec((None, CONV_W, FF_TILE), gcol), pl.BlockSpec((None, CONV_W, FF_TILE), vcol),
            pl.BlockSpec((None, 1, FF_TILE), gcol), pl.BlockSpec((None, 1, FF_TILE), vcol)]


def _ffn_cast_weight_specs(idx):
    col = lambda *g: (0, idx(*g))
    return [pl.BlockSpec((D_MODEL, FF_TILE), col), pl.BlockSpec((D_MODEL, FF_TILE), col),
            pl.BlockSpec((FF_TILE, D_MODEL), lambda *g: (idx(*g), 0))]


def _ffn_prompt_kernel(final_norm, tiles_per_seq,
                       x_ref, gn_ref, fg_ref, wug_ref, wuv_ref, wd_ref, cwg_ref, cwv_ref, cbg_ref, cbv_ref,
                       o_ref, cg_ref, cv_ref,
                       h_s, extg_s, extv_s, carryg_s, carryv_s):
    m = pl.program_id(0)
    f = pl.program_id(1)
    tm = x_ref.shape[0]

    @pl.when(f == 0)
    def _():
        x = x_ref[...]
        h_s[...] = _rms(x, gn_ref[...]).astype(BF16)
        o_ref[...] = x

    seq_start = (m % tiles_per_seq) == 0

    @pl.when(seq_start)
    def _():
        extg_s[0:CONV_PAD, :] = jnp.zeros((CONV_PAD, FF_TILE), F32)
        extv_s[0:CONV_PAD, :] = jnp.zeros((CONV_PAD, FF_TILE), F32)

    @pl.when(jnp.logical_not(seq_start))
    def _():
        extg_s[0:CONV_PAD, :] = carryg_s[f]
        extv_s[0:CONV_PAD, :] = carryv_s[f]

    def conv(ext_s, cw_ref, cb_ref, cs):
        out = cb_ref[:, cs]
        for t in range(CONV_W):
            lag = CONV_W - 1 - t
            out = out + cw_ref[t:t + 1, cs] * ext_s[CONV_PAD - lag:CONV_PAD - lag + tm, cs]
        return out

    subs = [slice(s * FF_SUB, (s + 1) * FF_SUB) for s in range(FF_TILE // FF_SUB)]
    for cs in subs:
        extg_s[CONV_PAD:, cs] = jnp.dot(h_s[...], wug_ref[:, cs], preferred_element_type=F32)
    for cs in subs:
        extv_s[CONV_PAD:, cs] = jnp.dot(h_s[...], wuv_ref[:, cs], preferred_element_type=F32)
    gates = [jax.nn.silu(conv(extg_s, cwg_ref, cbg_ref, cs)) for cs in subs]
    down = None
    for gate, cs in zip(gates, subs):
        act = (gate * conv(extv_s, cwv_ref, cbv_ref, cs)).astype(BF16)
        part = jnp.dot(act, wd_ref[cs, :], preferred_element_type=F32)
        down = part if down is None else down + part
    o_ref[...] += down

    carryg_s[f] = extg_s[tm:tm + CONV_PAD, :]
    carryv_s[f] = extv_s[tm:tm + CONV_PAD, :]
    cg_ref[f] = extg_s[tm + CONV_PAD - (CONV_W - 1):tm + CONV_PAD, :]
    cv_ref[f] = extv_s[tm + CONV_PAD - (CONV_W - 1):tm + CONV_PAD, :]

    if final_norm:
        @pl.when(f == pl.num_programs(1) - 1)
        def _():
            o_ref[...] = _rms(o_ref[...], fg_ref[...])


def _ffn_prompt(l, x, gn, wug, wuv, wd, cw, cb, fg, batch, seq, tm, final_norm):
    m = batch * seq
    tiles_per_seq = seq // tm
    row = lambda i, f: (i, 0)
    state = lambda i, f: (i // tiles_per_seq, 0, 0, 0)
    return pl.pallas_call(
        functools.partial(_ffn_prompt_kernel, final_norm, tiles_per_seq),
        grid=(m // tm, N_FF),
        in_specs=[pl.BlockSpec((tm, D_MODEL), row, pipeline_mode=pl.Buffered(1)),
                  _of_layer(l, (1, D_MODEL)), _resident((1, D_MODEL))]
                 + _ffn_cast_weight_specs(lambda i, f: f) + _ffn_conv_specs(l, lambda i, f: f),
        out_specs=[pl.BlockSpec((tm, D_MODEL), row),
                   pl.BlockSpec((None, N_FF, CONV_W - 1, FF_TILE), state),
                   pl.BlockSpec((None, N_FF, CONV_W - 1, FF_TILE), state)],
        out_shape=[jax.ShapeDtypeStruct((m, D_MODEL), F32),
                   jax.ShapeDtypeStruct((batch, N_FF, CONV_W - 1, FF_TILE), F32),
                   jax.ShapeDtypeStruct((batch, N_FF, CONV_W - 1, FF_TILE), F32)],
        scratch_shapes=[pltpu.VMEM((tm, D_MODEL), BF16),
                        pltpu.VMEM((tm + CONV_PAD, FF_TILE), F32),
                        pltpu.VMEM((tm + CONV_PAD, FF_TILE), F32),
                        pltpu.VMEM((N_FF, CONV_PAD, FF_TILE), F32),
                        pltpu.VMEM((N_FF, CONV_PAD, FF_TILE), F32)],
        compiler_params=_params(("arbitrary", "arbitrary"), 56),
        name="ffn_prompt",
    )(x, gn, fg, wug, wuv, wd, cw, cw, cb, cb)


CAST_ROWS = 256


def _cast_weight(w_ref, wb_ref):
    for r in range(0, w_ref.shape[0], CAST_ROWS):
        wb_ref[r:r + CAST_ROWS, :] = w_ref[r:r + CAST_ROWS, :].astype(BF16)


def _ffn_sample_kernel(final_norm,
                       x_ref, gn_ref, fg_ref, wug_ref, wuv_ref, wd_ref, cwg_ref, cwv_ref, cbg_ref, cbv_ref,
                       s0g_ref, s0v_ref, s1g_ref, s1v_ref,
                       o_ref, ug_ref, uv_ref, wugb_ref, wuvb_ref, wdb_ref, h_s):
    f = pl.program_id(0)

    @pl.when(f == 0)
    def _():
        x = x_ref[...]
        h_s[...] = _rms(x, gn_ref[...]).astype(BF16)
        o_ref[...] = x

    def conv(w_ref, wb_ref, cw_ref, cb_ref, s0_ref, s1_ref, up_ref):
        _cast_weight(w_ref, wb_ref)
        up = jnp.dot(h_s[...], wb_ref[...], preferred_element_type=F32)
        up_ref[...] = up
        return ((cb_ref[...] + cw_ref[0:1, :] * s0_ref[...]) + cw_ref[1:2, :] * s1_ref[...]) + cw_ref[2:3, :] * up

    gate = conv(wug_ref, wugb_ref, cwg_ref, cbg_ref, s0g_ref, s1g_ref, ug_ref)
    val = conv(wuv_ref, wuvb_ref, cwv_ref, cbv_ref, s0v_ref, s1v_ref, uv_ref)
    act = (jax.nn.silu(gate) * val).astype(BF16)
    _cast_weight(wd_ref, wdb_ref)
    o_ref[...] += jnp.dot(act, wdb_ref[...], preferred_element_type=F32)

    if final_norm:
        @pl.when(f == pl.num_programs(0) - 1)
        def _():
            o_ref[...] = _rms(o_ref[...], fg_ref[...])


def _ffn_sample(l, x, gn, wu, wd, cw, cb, fg, s0, s1, final_norm):
    n = x.shape[0]
    gcol = lambda f: (l, 0, f)
    vcol = lambda f: (l, 0, N_FF + f)
    st = lambda im: pl.BlockSpec((None, n, FF_TILE), im)
    return pl.pallas_call(
        functools.partial(_ffn_sample_kernel, final_norm),
        grid=(N_FF,),
        in_specs=[_resident((n, D_MODEL)), _of_layer(l, (1, D_MODEL)), _resident((1, D_MODEL)),
                  pl.BlockSpec((None, D_MODEL, FF_TILE), gcol), pl.BlockSpec((None, D_MODEL, FF_TILE), vcol),
                  pl.BlockSpec((None, FF_TILE, D_MODEL), lambda f: (l, f, 0))]
                 + _ffn_conv_specs(l, lambda f: f)
                 + [st(gcol), st(vcol), st(gcol), st(vcol)],
        out_specs=[pl.BlockSpec((n, D_MODEL), lambda f: (0, 0)),
                   pl.BlockSpec((n, FF_TILE), lambda f: (0, f)),
                   pl.BlockSpec((n, FF_TILE), lambda f: (0, f))]
                  + _ffn_cast_weight_specs(lambda f: f),
        out_shape=[jax.ShapeDtypeStruct((n, D_MODEL), F32),
                   jax.ShapeDtypeStruct((n, D_FF), F32),
                   jax.ShapeDtypeStruct((n, D_FF), F32),
                   jax.ShapeDtypeStruct((D_MODEL, D_FF), BF16),
                   jax.ShapeDtypeStruct((D_MODEL, D_FF), BF16),
                   jax.ShapeDtypeStruct((D_FF, D_MODEL), BF16)],
        scratch_shapes=[pltpu.VMEM((n, D_MODEL), BF16)],
        compiler_params=_params(("arbitrary",), 44),
        name="ffn_sample",
    )(x, gn, fg, wu, wu, wd, cw, cw, cb, cb, s0, s0, s1, s1)


def _rope_tables(pos):
    half = HEAD_DIM // 2
    inv = ROPE_THETA ** (-jnp.arange(half, dtype=F32) / half)
    ang = pos.astype(F32)[:, None] * inv[None, :]
    cos, sin = jnp.cos(ang), jnp.sin(ang)
    reps = LANES // HEAD_DIM
    return (jnp.concatenate([cos, cos] * reps, axis=1),
            jnp.concatenate([-sin, sin] * reps, axis=1))


def _ssm_params(a_re, a_im, b_re, b_im, c_re, c_im, d, log_dt):
    depth = a_re.shape[0]
    dt = jnp.exp(log_dt)[..., None]
    mag = jnp.exp(a_re * dt)
    lr, li = mag * jnp.cos(a_im * dt), mag * jnp.sin(a_im * dt)
    nr, ni = lr - 1.0, li
    den = a_re * a_re + a_im * a_im
    qr = (nr * a_re + ni * a_im) / den
    qi = (ni * a_re - nr * a_im) / den
    bbr = qr[..., None] * b_re - qi[..., None] * b_im
    bbi = qr[..., None] * b_im + qi[..., None] * b_re

    def compact(t):
        r, c = t.shape[2], t.shape[3]
        t = t.reshape(depth, N_CHUNK, GROUP_CHUNK * r, c).astype(BF16)
        return jnp.pad(t, ((0, 0), (0, 0), (0, 0), (0, LANES - c)))

    def lane_tiling(c):
        src = jnp.arange(LANES)[:, None]
        dst = jnp.arange(GROUP_CHUNK * c)[None, :] % c
        return (src == dst).astype(BF16)

    return dict(bre=compact(bbr.transpose(0, 1, 3, 2)), bim=compact(bbi.transpose(0, 1, 3, 2)),
                cre=compact(c_re.transpose(0, 1, 3, 2)), cimn=compact((-c_im).transpose(0, 1, 3, 2)),
                tile_b=lane_tiling(SSM_STATE), tile_c=lane_tiling(SSM_GROUP),
                lre=lr.reshape(depth, 1, N_STATE), lim=li.reshape(depth, 1, N_STATE),
                d=d.reshape(depth, 1, SSM_WIDTH))


def _segment_permutation():
    r = jnp.arange(SSM_TILE)
    src = (r % SUBLANES) * SEG_LEN + r // SUBLANES
    return (src[:, None] == jnp.arange(SSM_TILE)[None, :]).astype(BF16)


def kernel(x_prompt, x_sample, cache_k, cache_v, state_ssm_re, state_ssm_im, state_conv, attn_norm_g, w_in, attn_sinks, ssm_a_re, ssm_a_im, ssm_b_re, ssm_b_im, ssm_c_re, ssm_c_im, ssm_d, ssm_log_dt, w_glu, b_glu, attn_out_norm_g, ssm_out_norm_g, w_out, ffn_norm_g, w_up, conv_w, conv_b, w_down, final_norm_g):
    batch, seq, _ = x_prompt.shape
    nsamp, dec_seq, _ = x_sample.shape
    wbuf = cache_k.shape[2]
    assert dec_seq == 1 and wbuf == WINDOW and seq % SSM_TILE == 0
    assert PAST_LEN >= wbuf

    w_in_b, w_glu_b, w_out_b = w_in.astype(BF16), w_glu.astype(BF16), w_out.astype(BF16)
    ssm = _ssm_params(ssm_a_re, ssm_a_im, ssm_b_re, ssm_b_im, ssm_c_re, ssm_c_im, ssm_d, ssm_log_dt)
    perm = _segment_permutation()

    cos_p, sin_p = _rope_tables(jnp.arange(seq, dtype=jnp.int32))
    cos_s, sin_s = _rope_tables(jnp.full((nsamp,), PAST_LEN, dtype=jnp.int32))

    rows = lambda a: a.reshape(a.shape[0], 1, a.shape[-1])
    g_in, g_a, g_s, g_f = rows(attn_norm_g), rows(attn_out_norm_g), rows(ssm_out_norm_g), rows(ffn_norm_g)
    bg, cb = rows(b_glu), rows(conv_b)
    fg = final_norm_g.reshape(1, D_MODEL)
    sink_col = attn_sinks.reshape(DEPTH, N_HEADS, 1)
    head_is_lo = (jnp.arange(N_HEADS) < KV_REP)[None, :, None]
    kc = cache_k.reshape(DEPTH, nsamp, wbuf, KV_COLS)
    vc = cache_v.reshape(DEPTH, nsamp, wbuf, KV_COLS)
    h0re = state_ssm_re.reshape(DEPTH, nsamp, N_STATE)
    h0im = state_ssm_im.reshape(DEPTH, nsamp, N_STATE)
    conv0, conv1 = state_conv[:, :, 0, :], state_conv[:, :, 1, :]

    xp = x_prompt.reshape(batch * seq, D_MODEL)
    xs = x_sample.reshape(nsamp, D_MODEL)
    tm_p = 512
    tm_ffn = 1024
    outs = {k: [] for k in ("kp", "vp", "hrp", "hip", "cgp", "cvp", "ks", "vs", "hrs", "his", "ugs", "uvs")}

    for l in range(DEPTH):
        last = l == DEPTH - 1

        q, k, v, u = _in_proj(l, xs, g_in, w_in_b, cos_s, sin_s, nsamp)
        qh = q.reshape(nsamp, N_HEADS, HEAD_DIM)
        zq = jnp.zeros_like(qh)
        qbd = jnp.where(head_is_lo, jnp.concatenate([qh, zq], axis=-1), jnp.concatenate([zq, qh], axis=-1))
        o = _attn_sample(l, qbd, kc, vc, k.reshape(nsamp, 1, KV_COLS), v.reshape(nsamp, 1, KV_COLS), sink_col, 8)
        a_raw = jnp.concatenate([o[:, :KV_REP, :HEAD_DIM].reshape(nsamp, -1),
                                 o[:, KV_REP:, HEAD_DIM:].reshape(nsamp, -1)], axis=1)
        a = _rms_cast(l, a_raw, g_a)
        s, hre, him = _ssm_sample(l, u, h0re, h0im, ssm, w_glu_b, bg, g_s)
        x1 = _out_proj(l, xs, a, s, w_out_b, nsamp)
        xs, ug, uv, wug_b, wuv_b, wd_b = _ffn_sample(l, x1, g_f, w_up, w_down, conv_w, cb, fg, conv0, conv1, last)
        outs["ks"].append(k)
        outs["vs"].append(v)
        outs["hrs"].append(hre)
        outs["his"].append(him)
        outs["ugs"].append(ug)
        outs["uvs"].append(uv)

        q, k, v, u = _in_proj(l, xp, g_in, w_in_b, cos_p, sin_p, tm_p)
        a = _attn_prompt(l, q, k, v, attn_sinks, g_a, batch, seq)
        s, hre, him = _ssm_prompt(l, u, perm, ssm, w_glu_b, bg, g_s, batch, seq)
        x1 = _out_proj(l, xp, a, s, w_out_b, tm_p)
        xp, cg, cv = _ffn_prompt(l, x1, g_f, wug_b, wuv_b, wd_b, conv_w, cb, fg, batch, seq, tm_ffn, last)
        outs["kp"].append(k.reshape(batch, seq, KV_COLS)[:, seq - WINDOW:])
        outs["vp"].append(v.reshape(batch, seq, KV_COLS)[:, seq - WINDOW:])
        outs["hrp"].append(hre)
        outs["hip"].append(him)
        outs["cgp"].append(cg)
        outs["cvp"].append(cv)

    st = lambda name: jnp.stack(outs[name], axis=0)
    heads = lambda t: t.reshape(t.shape[:-1] + (N_KV_HEADS, HEAD_DIM))
    states = lambda t: t.reshape(DEPTH, -1, SSM_GROUPS, SSM_STATE)
    unblock = lambda t: t.transpose(0, 1, 3, 2, 4).reshape(DEPTH, batch, CONV_W - 1, D_FF)
    conv_prompt = jnp.concatenate([unblock(st("cgp")), unblock(st("cvp"))], axis=-1)
    k_sample = jnp.concatenate([cache_k[:, :, 1:], heads(st("ks"))[:, :, None]], axis=2)
    v_sample = jnp.concatenate([cache_v[:, :, 1:], heads(st("vs"))[:, :, None]], axis=2)
    conv_sample = jnp.stack([conv1, jnp.concatenate([st("ugs"), st("uvs")], axis=-1)], axis=2)
    return (xp.reshape(batch, seq, D_MODEL), xs.reshape(nsamp, 1, D_MODEL),
            heads(st("kp")), heads(st("vp")), states(st("hrp")), states(st("hip")), conv_prompt,
            k_sample, v_sample, states(st("hrs")), states(st("his")), conv_sample)
```

```python
import functools

import jax
import jax.numpy as jnp
from jax import lax
from jax.experimental import pallas as pl
from jax.experimental.pallas import tpu as pltpu

F32 = jnp.float32
BF16 = jnp.bfloat16

D_MODEL = 2048
DEPTH = 4
ATTN_WIDTH = 1024
SSM_WIDTH = 1024
HEAD_DIM = 64
N_HEADS = 16
N_KV_HEADS = 2
KV_REP = 8
KV_COLS = N_KV_HEADS * HEAD_DIM
WINDOW = 128
ROPE_THETA = 10000.0
SSM_GROUP = 16
SSM_GROUPS = 64
SSM_STATE = 64
N_STATE = SSM_GROUPS * SSM_STATE
D_FF = 5632
CONV_W = 3
RMS_EPS = 1e-6
IN_COLS = ATTN_WIDTH + 2 * KV_COLS + SSM_WIDTH
NEG_INF = -1e30
PAST_LEN = 16384
SCORE_SCALE = HEAD_DIM ** -0.5
assert SCORE_SCALE == 2.0 ** -3

LANES = 128
SUBLANES = 8
MXU_DIM = 256
MIB = 1024 * 1024
U_BLOCKS = SSM_WIDTH // LANES

GROUP_CHUNK = 16
N_CHUNK = SSM_GROUPS // GROUP_CHUNK
CH_U = GROUP_CHUNK * SSM_GROUP
CH_S = GROUP_CHUNK * SSM_STATE
BLOCKS_PER_CHUNK = CH_U // LANES

SEG_LEN = 32
SSM_TILE = SUBLANES * SEG_LEN
SSM_SUBTILES = 2
SCAN_LANES = 512


def _rms(x, g):
    ms = jnp.mean(x * x, axis=-1, keepdims=True)
    return x * lax.rsqrt(ms + RMS_EPS) * g


def _params(sem, vmem_mib):
    return pltpu.CompilerParams(dimension_semantics=sem, vmem_limit_bytes=vmem_mib * MIB)


def _resident(shape):
    nd = len(shape)
    return pl.BlockSpec(shape, lambda *_: (0,) * nd, pipeline_mode=pl.Buffered(1))


def _of_layer(l, shape):
    nd = len(shape)
    return pl.BlockSpec((None,) + tuple(shape), lambda *_: (l,) + (0,) * nd, pipeline_mode=pl.Buffered(1))


def _in_proj_kernel(x_ref, g_ref, w_ref, cos_ref, sin_ref, q_ref, k_ref, v_ref, u_ref):
    h = _rms(x_ref[...], g_ref[...]).astype(BF16)
    proj = jnp.dot(h, w_ref[...], preferred_element_type=F32)
    cos = cos_ref[...]
    sin = sin_ref[...]
    lane = lax.broadcasted_iota(jnp.int32, cos.shape, 1)
    first_half = (lane & (HEAD_DIM - 1)) < (HEAD_DIM // 2)

    def rope(blk):
        partner = jnp.where(first_half,
                            pltpu.roll(blk, LANES - HEAD_DIM // 2, 1),
                            pltpu.roll(blk, HEAD_DIM // 2, 1))
        return blk * cos + partner * sin

    for j in range(ATTN_WIDTH // LANES):
        q_ref[:, j * LANES:(j + 1) * LANES] = (rope(proj[:, j * LANES:(j + 1) * LANES]) * SCORE_SCALE).astype(BF16)
    k_ref[...] = rope(proj[:, ATTN_WIDTH:ATTN_WIDTH + KV_COLS])
    v_ref[...] = proj[:, ATTN_WIDTH + KV_COLS:ATTN_WIDTH + 2 * KV_COLS]
    u0 = ATTN_WIDTH + 2 * KV_COLS
    for j in range(U_BLOCKS):
        u_ref[j] = proj[:, u0 + j * LANES:u0 + (j + 1) * LANES]


def _in_proj(l, x, g, w, cos, sin, tm):
    m = x.shape[0]
    pos_tiles = cos.shape[0] // tm
    row = lambda i: (i, 0)
    return pl.pallas_call(
        _in_proj_kernel,
        grid=(m // tm,),
        in_specs=[pl.BlockSpec((tm, D_MODEL), row),
                  _of_layer(l, (1, D_MODEL)),
                  _of_layer(l, (D_MODEL, IN_COLS)),
                  pl.BlockSpec((tm, LANES), lambda i: (i % pos_tiles, 0)),
                  pl.BlockSpec((tm, LANES), lambda i: (i % pos_tiles, 0))],
        out_specs=[pl.BlockSpec((tm, ATTN_WIDTH), row),
                   pl.BlockSpec((tm, KV_COLS), row),
                   pl.BlockSpec((tm, KV_COLS), row),
                   pl.BlockSpec((U_BLOCKS, tm, LANES), lambda i: (0, i, 0))],
        out_shape=[jax.ShapeDtypeStruct((m, ATTN_WIDTH), BF16),
                   jax.ShapeDtypeStruct((m, KV_COLS), F32),
                   jax.ShapeDtypeStruct((m, KV_COLS), F32),
                   jax.ShapeDtypeStruct((U_BLOCKS, m, LANES), F32)],
        compiler_params=_params(("arbitrary",), 48),
        name="in_proj",
    )(x, g, w, cos, sin)


ATTN_SUB = 4


def _attn_prompt_kernel(layer, sinks_ref, q_ref, kc_ref, kp_ref, vc_ref, vp_ref, gn_ref, a_ref):
    j = pl.program_id(1)
    lane = lax.broadcasted_iota(jnp.int32, (WINDOW, LANES), 1)
    lo = lane < HEAD_DIM
    qi = lax.broadcasted_iota(jnp.int32, (WINDOW, 2 * WINDOW), 0)
    kj = lax.broadcasted_iota(jnp.int32, (WINDOW, 2 * WINDOW), 1)
    diff = qi + WINDOW - kj
    band = (diff >= 0) & (diff <= WINDOW)
    first_key = jnp.where(j > 0, 0, WINDOW)
    valid = [band & (kj >= first_key)] + [band] * (ATTN_SUB - 1)
    zero = jnp.zeros((WINDOW, LANES), F32)

    def halves(x, g):
        r = pltpu.roll(x, HEAD_DIM, 1)
        if g == 0:
            return jnp.where(lo, x, zero).astype(BF16), jnp.where(lo, zero, r).astype(BF16)
        return jnp.where(lo, r, zero).astype(BF16), jnp.where(lo, zero, x).astype(BF16)

    rows = lambda sb: slice(sb * WINDOW, (sb + 1) * WINDOW)
    k_blocks = [kp_ref[...]] + [kc_ref[rows(sb), :] for sb in range(ATTN_SUB)]
    v_blocks = [vp_ref[...]] + [vc_ref[rows(sb), :] for sb in range(ATTN_SUB)]
    k_halves = [[halves(x, g) for x in k_blocks] for g in range(N_KV_HEADS)]
    v_halves = [[halves(x, g) for x in v_blocks] for g in range(N_KV_HEADS)]
    pairs = KV_REP // 2
    nt = (((1,), (1,)), ((), ()))

    for sb in range(ATTN_SUB):
        def softmax(s, head, sb=sb):
            s = jnp.where(valid[sb], s, NEG_INF)
            sink = sinks_ref[layer, head]
            m = jnp.maximum(jnp.max(s, axis=-1, keepdims=True), sink)
            p = jnp.exp(s - m)
            denom = jnp.sum(p, axis=-1, keepdims=True) + jnp.exp(sink - m)
            return (p / denom).astype(BF16)

        out_blocks = []
        for g in range(N_KV_HEADS):
            window = lambda hv, half: jnp.concatenate([hv[g][sb][half], hv[g][sb + 1][half]], axis=0)
            k_lo, k_hi, v_lo, v_hi = window(k_halves, 0), window(k_halves, 1), window(v_halves, 0), window(v_halves, 1)
            qg = jnp.concatenate([q_ref[rows(sb), (g * pairs + p) * LANES:(g * pairs + p + 1) * LANES]
                                  for p in range(pairs)], axis=0)
            s_even = lax.dot_general(qg, k_lo, nt, preferred_element_type=F32)
            s_odd = lax.dot_general(qg, k_hi, nt, preferred_element_type=F32)
            p_even = jnp.concatenate(
                [softmax(s_even[rows(p)], g * KV_REP + 2 * p) for p in range(pairs)], axis=0)
            p_odd = jnp.concatenate(
                [softmax(s_odd[rows(p)], g * KV_REP + 2 * p + 1) for p in range(pairs)], axis=0)
            o = (jnp.dot(p_even, v_lo, preferred_element_type=F32)
                 + jnp.dot(p_odd, v_hi, preferred_element_type=F32))
            out_blocks += [o[rows(p)] for p in range(pairs)]

        ssq = out_blocks[0] * out_blocks[0]
        for blk in out_blocks[1:]:
            ssq = ssq + blk * blk
        inv = lax.rsqrt(jnp.sum(ssq, axis=-1, keepdims=True) / ATTN_WIDTH + RMS_EPS)
        for i, blk in enumerate(out_blocks):
            cols = slice(i * LANES, (i + 1) * LANES)
            a_ref[rows(sb), cols] = (blk * inv * gn_ref[:, cols]).astype(BF16)


def _attn_prompt(l, q, k, v, sinks, gn, batch, seq):
    nb = seq // WINDOW
    steps = nb // ATTN_SUB
    cur = lambda b, j: (b * steps + j, 0)
    prev = lambda b, j: (b * nb + jnp.maximum(j * ATTN_SUB - 1, 0), 0)
    return pl.pallas_call(
        functools.partial(_attn_prompt_kernel, l),
        grid=(batch, steps),
        in_specs=[pl.BlockSpec(memory_space=pltpu.SMEM),
                  pl.BlockSpec((ATTN_SUB * WINDOW, ATTN_WIDTH), cur),
                  pl.BlockSpec((ATTN_SUB * WINDOW, KV_COLS), cur), pl.BlockSpec((WINDOW, KV_COLS), prev),
                  pl.BlockSpec((ATTN_SUB * WINDOW, KV_COLS), cur), pl.BlockSpec((WINDOW, KV_COLS), prev),
                  _of_layer(l, (1, ATTN_WIDTH))],
        out_specs=pl.BlockSpec((ATTN_SUB * WINDOW, ATTN_WIDTH), cur),
        out_shape=jax.ShapeDtypeStruct((batch * seq, ATTN_WIDTH), BF16),
        compiler_params=_params(("arbitrary", "arbitrary"), 32),
        name="attn_prompt",
    )(sinks, q, k, k, v, v, gn)


def _attn_sample_kernel(qbd_ref, kc_ref, vc_ref, kn_ref, vn_ref, sink_ref, o_ref):
    bb = qbd_ref.shape[0]
    rows, keys = bb * N_HEADS, bb * WINDOW
    q = qbd_ref[...].reshape(rows, LANES)
    kc = kc_ref[...].reshape(keys, KV_COLS).astype(BF16)
    vc = vc_ref[...].reshape(keys, KV_COLS).astype(BF16)
    per_head = lambda ref: jnp.broadcast_to(ref[...], (bb, N_HEADS, KV_COLS)).reshape(rows, KV_COLS)
    kn, vn = per_head(kn_ref), per_head(vn_ref)
    sink = jnp.concatenate([sink_ref[...]] * bb, axis=0)
    nt = (((1,), (1,)), ((), ()))
    s = lax.dot_general(q, kc, nt, preferred_element_type=F32)
    own = ((lax.broadcasted_iota(jnp.int32, s.shape, 0) >> _log2(N_HEADS))
           == (lax.broadcasted_iota(jnp.int32, s.shape, 1) >> _log2(WINDOW)))
    s = jnp.where(own, s, NEG_INF)
    s_new = jnp.sum(q.astype(F32) * kn, axis=-1, keepdims=True)
    m = jnp.maximum(jnp.maximum(jnp.max(s, axis=-1, keepdims=True), s_new), sink)
    p = jnp.exp(s - m)
    p_new = jnp.exp(s_new - m)
    denom = jnp.sum(p, axis=-1, keepdims=True) + p_new + jnp.exp(sink - m)
    o = jnp.dot((p / denom).astype(BF16), vc, preferred_element_type=F32) + (p_new / denom) * vn
    o_ref[...] = o.reshape(bb, N_HEADS, LANES)


def _attn_sample(l, qbd, kc, vc, kn, vn, sink_col, bb):
    nbatch = qbd.shape[0]
    blk = lambda *s: pl.BlockSpec((bb,) + s, lambda i: (i, 0, 0))
    cache = pl.BlockSpec((None, bb, WINDOW, KV_COLS), lambda i: (l, i, 0, 0))
    return pl.pallas_call(
        _attn_sample_kernel,
        grid=(nbatch // bb,),
        in_specs=[blk(N_HEADS, LANES), cache, cache,
                  blk(1, KV_COLS), blk(1, KV_COLS), _of_layer(l, (N_HEADS, 1))],
        out_specs=blk(N_HEADS, LANES),
        out_shape=jax.ShapeDtypeStruct((nbatch, N_HEADS, LANES), F32),
        compiler_params=_params(("arbitrary",), 32),
        name="attn_sample",
    )(qbd, kc, vc, kn, vn, sink_col)


def _rms_cast_kernel(x_ref, g_ref, o_ref):
    o_ref[...] = _rms(x_ref[...], g_ref[...]).astype(BF16)


def _rms_cast(l, x, g):
    n, d = x.shape
    return pl.pallas_call(
        _rms_cast_kernel,
        grid=(1,),
        in_specs=[_resident((n, d)), _of_layer(l, (1, d))],
        out_specs=pl.BlockSpec((n, d), lambda i: (0, 0)),
        out_shape=jax.ShapeDtypeStruct(x.shape, BF16),
        name="rms_cast",
    )(x, g)


def _glu_norm(z, wg_ref, bg_ref, gn_ref):
    gate = jax.nn.sigmoid(jnp.dot(z.astype(BF16), wg_ref[...], preferred_element_type=F32) + bg_ref[...])
    return _rms(z * gate, gn_ref[...])


def _ssm_operands(l, ssm, wg, bg, gn):
    layered = (ssm["bre"], ssm["bim"], ssm["cre"], ssm["cimn"], ssm["lre"], ssm["lim"], ssm["d"], wg, bg, gn)
    shared = (ssm["tile_b"], ssm["tile_c"])
    return (layered + shared,
            [_of_layer(l, a.shape[1:]) for a in layered] + [_resident(a.shape) for a in shared])


def _log2(n):
    assert n & (n - 1) == 0
    return n.bit_length() - 1


def _block_diag_scratch():
    return [pltpu.VMEM((N_CHUNK, CH_U, CH_S), BF16), pltpu.VMEM((N_CHUNK, CH_U, CH_S), BF16),
            pltpu.VMEM((N_CHUNK, CH_S, CH_U), BF16), pltpu.VMEM((N_CHUNK, CH_S, CH_U), BF16)]


def _expand_block_diag(t_ref, tile_ref, out_s, rows_per_group, cols_per_group):
    for c in range(N_CHUNK):
        tiled = jnp.dot(t_ref[c], tile_ref[...], preferred_element_type=F32)
        row_group = lax.broadcasted_iota(jnp.int32, tiled.shape, 0) >> _log2(rows_per_group)
        col_group = lax.broadcasted_iota(jnp.int32, tiled.shape, 1) >> _log2(cols_per_group)
        out_s[c] = jnp.where(row_group == col_group, tiled, 0.0).astype(BF16)


def _expand_projections(tb_re, tb_im, tc_re, tc_imn, tile_b, tile_c, bre_s, bim_s, cre_s, cimn_s):
    _expand_block_diag(tb_re, tile_b, bre_s, SSM_GROUP, SSM_STATE)
    _expand_block_diag(tb_im, tile_b, bim_s, SSM_GROUP, SSM_STATE)
    _expand_block_diag(tc_re, tile_c, cre_s, SSM_STATE, SSM_GROUP)
    _expand_block_diag(tc_imn, tile_c, cimn_s, SSM_STATE, SSM_GROUP)


def _ssm_prompt_kernel(u_ref, perm_ref, tb_re, tb_im, tc_re, tc_imn, lre_ref, lim_ref, d_ref,
                       wg_ref, bg_ref, gn_ref, tile_b, tile_c,
                       s_ref, hre_ref, him_ref,
                       bre_ref, bim_ref, cre_ref, cimn_ref,
                       xre_s, xim_s, hbre_s, hbim_s, pre_s, pim_s, cre_s, cim_s, hinre_s, hinim_s, y_s):
    b = pl.program_id(0)
    j = pl.program_id(1)

    @pl.when((b == 0) & (j == 0))
    def _():
        _expand_projections(tb_re, tb_im, tc_re, tc_imn, tile_b, tile_c, bre_ref, bim_ref, cre_ref, cimn_ref)
        lr, li = lre_ref[...], lim_ref[...]
        pr, pi = lr, li
        for i in range(SEG_LEN):
            if i:
                pr, pi = pr * lr - pi * li, pr * li + pi * lr
            pre_s[i * SUBLANES:(i + 1) * SUBLANES, :] = jnp.broadcast_to(pr, (SUBLANES, N_STATE))
            pim_s[i * SUBLANES:(i + 1) * SUBLANES, :] = jnp.broadcast_to(pi, (SUBLANES, N_STATE))

    @pl.when(j == 0)
    def _():
        cre_s[...] = jnp.zeros_like(cre_s)
        cim_s[...] = jnp.zeros_like(cim_s)

    u_nat = jnp.concatenate([u_ref[jb] for jb in range(U_BLOCKS)], axis=1)
    u_b = u_nat.astype(BF16)
    u_perm = jnp.concatenate(
        [jnp.dot(perm_ref[...], u_b[t * SSM_TILE:(t + 1) * SSM_TILE], preferred_element_type=F32).astype(BF16)
         for t in range(SSM_SUBTILES)], axis=0)

    last = (SEG_LEN - 1) * SUBLANES
    for c in range(N_CHUNK):
        uc = u_perm[:, c * CH_U:(c + 1) * CH_U]
        chunk = slice(c * CH_S, (c + 1) * CH_S)
        xre_s[:, chunk] = jnp.dot(uc, bre_ref[c], preferred_element_type=F32)
        xim_s[:, chunk] = jnp.dot(uc, bim_ref[c], preferred_element_type=F32)

        for t in range(SSM_SUBTILES):
            for lc in range(CH_S // SCAN_LANES):
                sl = slice(c * CH_S + lc * SCAN_LANES, c * CH_S + (lc + 1) * SCAN_LANES)
                lr = jnp.broadcast_to(lre_ref[:, sl], (SUBLANES, SCAN_LANES))
                li = jnp.broadcast_to(lim_ref[:, sl], (SUBLANES, SCAN_LANES))
                r0 = t * SSM_TILE
                hr = xre_s[r0:r0 + SUBLANES, sl]
                hi = xim_s[r0:r0 + SUBLANES, sl]
                for i in range(1, SEG_LEN):
                    rows = slice(r0 + i * SUBLANES, r0 + (i + 1) * SUBLANES)
                    hr, hi = (lr * hr - li * hi) + xre_s[rows, sl], (lr * hi + li * hr) + xim_s[rows, sl]
                    xre_s[rows, sl] = hr
                    xim_s[rows, sl] = hi

        l32r = pre_s[last:last + 1, chunk]
        l32i = pim_s[last:last + 1, chunk]
        cr, ci = cre_s[:, chunk], cim_s[:, chunk]
        for seg in range(SSM_SUBTILES * SUBLANES):
            hinre_s[seg:seg + 1, chunk] = cr
            hinim_s[seg:seg + 1, chunk] = ci
            end_row = (seg // SUBLANES) * SSM_TILE + last + seg % SUBLANES
            er = xre_s[end_row:end_row + 1, chunk]
            ei = xim_s[end_row:end_row + 1, chunk]
            cr, ci = (l32r * cr - l32i * ci) + er, (l32r * ci + l32i * cr) + ei
        cre_s[:, chunk] = cr
        cim_s[:, chunk] = ci

        for t in range(SSM_SUBTILES):
            for lc in range(CH_S // SCAN_LANES):
                sl = slice(c * CH_S + lc * SCAN_LANES, c * CH_S + (lc + 1) * SCAN_LANES)
                hr_in = jnp.concatenate([hinre_s[t * SUBLANES:(t + 1) * SUBLANES, sl]] * 2, axis=0)
                hi_in = jnp.concatenate([hinim_s[t * SUBLANES:(t + 1) * SUBLANES, sl]] * 2, axis=0)
                for i2 in range(SEG_LEN // 2):
                    prow = slice(i2 * 2 * SUBLANES, (i2 + 1) * 2 * SUBLANES)
                    rows = slice(t * SSM_TILE + prow.start, t * SSM_TILE + prow.stop)
                    pr, pi = pre_s[prow, sl], pim_s[prow, sl]
                    hbre_s[rows, sl] = (xre_s[rows, sl] + (pr * hr_in - pi * hi_in)).astype(BF16)
                    hbim_s[rows, sl] = (xim_s[rows, sl] + (pr * hi_in + pi * hr_in)).astype(BF16)

        y = (jnp.dot(hbre_s[:, chunk], cre_ref[c], preferred_element_type=F32)
             + jnp.dot(hbim_s[:, chunk], cimn_ref[c], preferred_element_type=F32))
        for h in range(BLOCKS_PER_CHUNK):
            y_s[c * BLOCKS_PER_CHUNK + h] = y[:, h * LANES:(h + 1) * LANES]

    y_nat = jnp.concatenate(
        [jnp.concatenate([y_s[jb, pl.ds(t * SSM_TILE + i0 * SUBLANES + seg, SUBLANES, stride=SUBLANES), :]
                          for t in range(SSM_SUBTILES)
                          for seg in range(SUBLANES) for i0 in range(0, SEG_LEN, SUBLANES)], axis=0)
         for jb in range(U_BLOCKS)], axis=1)
    z = jax.nn.gelu(y_nat + d_ref[...] * u_nat)
    s_ref[...] = _glu_norm(z, wg_ref, bg_ref, gn_ref).astype(BF16)

    @pl.when(j == pl.num_programs(1) - 1)
    def _():
        hre_ref[...] = cre_s[...]
        him_ref[...] = cim_s[...]


def _ssm_prompt(l, u, perm, ssm, wg, bg, gn, batch, seq):
    step_rows = SSM_SUBTILES * SSM_TILE
    nt = seq // step_rows
    row = lambda b, j: (b * nt + j, 0)
    st = lambda b, j: (b, 0, 0)
    arrays, specs = _ssm_operands(l, ssm, wg, bg, gn)
    scratch = _block_diag_scratch() + [
        pltpu.VMEM((step_rows, N_STATE), F32),
        pltpu.VMEM((step_rows, N_STATE), F32),
        pltpu.VMEM((step_rows, N_STATE), BF16),
        pltpu.VMEM((step_rows, N_STATE), BF16),
        pltpu.VMEM((SSM_TILE, N_STATE), F32),
        pltpu.VMEM((SSM_TILE, N_STATE), F32),
        pltpu.VMEM((1, N_STATE), F32),
        pltpu.VMEM((1, N_STATE), F32),
        pltpu.VMEM((SSM_SUBTILES * SUBLANES, N_STATE), F32),
        pltpu.VMEM((SSM_SUBTILES * SUBLANES, N_STATE), F32),
        pltpu.VMEM((U_BLOCKS, step_rows, LANES), F32),
    ]
    return pl.pallas_call(
        _ssm_prompt_kernel,
        grid=(batch, nt),
        in_specs=[pl.BlockSpec((U_BLOCKS, step_rows, LANES), lambda b, j: (0, b * nt + j, 0)),
                  _resident((SSM_TILE, SSM_TILE))] + specs,
        out_specs=[pl.BlockSpec((step_rows, SSM_WIDTH), row),
                   pl.BlockSpec((None, 1, N_STATE), st),
                   pl.BlockSpec((None, 1, N_STATE), st)],
        out_shape=[jax.ShapeDtypeStruct((batch * seq, SSM_WIDTH), BF16),
                   jax.ShapeDtypeStruct((batch, 1, N_STATE), F32),
                   jax.ShapeDtypeStruct((batch, 1, N_STATE), F32)],
        scratch_shapes=scratch,
        compiler_params=_params(("arbitrary", "arbitrary"), 56),
        name="ssm_prompt",
    )(u, perm, *arrays)


def _ssm_sample_kernel(u_ref, h0re_ref, h0im_ref, tb_re, tb_im, tc_re, tc_imn, lre_ref, lim_ref, d_ref,
                       wg_ref, bg_ref, gn_ref, tile_b, tile_c, s_ref, hre_ref, him_ref,
                       bre_ref, bim_ref, cre_ref, cimn_ref):
    _expand_projections(tb_re, tb_im, tc_re, tc_imn, tile_b, tile_c, bre_ref, bim_ref, cre_ref, cimn_ref)
    u = jnp.concatenate([u_ref[jb] for jb in range(U_BLOCKS)], axis=1)
    zs = []
    for c in range(N_CHUNK):
        cs = slice(c * CH_U, (c + 1) * CH_U)
        ss = slice(c * CH_S, (c + 1) * CH_S)
        uc = u[:, cs].astype(BF16)
        lr, li = lre_ref[:, ss], lim_ref[:, ss]
        h0r, h0i = h0re_ref[:, ss], h0im_ref[:, ss]
        hr = jnp.dot(uc, bre_ref[c], preferred_element_type=F32) + (lr * h0r - li * h0i)
        hi = jnp.dot(uc, bim_ref[c], preferred_element_type=F32) + (lr * h0i + li * h0r)
        hre_ref[:, ss] = hr
        him_ref[:, ss] = hi
        y = (jnp.dot(hr.astype(BF16), cre_ref[c], preferred_element_type=F32)
             + jnp.dot(hi.astype(BF16), cimn_ref[c], preferred_element_type=F32))
        zs.append(jax.nn.gelu(y + d_ref[:, cs] * u[:, cs]))
    z = jnp.concatenate(zs, axis=1)
    s_ref[...] = _glu_norm(z, wg_ref, bg_ref, gn_ref).astype(BF16)


def _ssm_sample(l, u, h0re, h0im, ssm, wg, bg, gn):
    n = u.shape[1]
    arrays, specs = _ssm_operands(l, ssm, wg, bg, gn)
    whole = lambda *s: pl.BlockSpec(s, lambda i: (0,) * len(s))
    return pl.pallas_call(
        _ssm_sample_kernel,
        grid=(1,),
        in_specs=[_resident((U_BLOCKS, n, LANES)), _of_layer(l, (n, N_STATE)), _of_layer(l, (n, N_STATE))] + specs,
        out_specs=[whole(n, SSM_WIDTH), whole(n, N_STATE), whole(n, N_STATE)],
        out_shape=[jax.ShapeDtypeStruct((n, SSM_WIDTH), BF16),
                   jax.ShapeDtypeStruct((n, N_STATE), F32),
                   jax.ShapeDtypeStruct((n, N_STATE), F32)],
        scratch_shapes=_block_diag_scratch(),
        compiler_params=_params(("arbitrary",), 40),
        name="ssm_sample",
    )(u, h0re, h0im, *arrays)


def _out_proj_kernel(x_ref, a_ref, s_ref, w_ref, o_ref):
    acc = (jnp.dot(a_ref[...], w_ref[0:ATTN_WIDTH, :], preferred_element_type=F32)
           + jnp.dot(s_ref[...], w_ref[ATTN_WIDTH:, :], preferred_element_type=F32))
    o_ref[...] = x_ref[...] + acc


def _out_proj(l, x, a, s, w, tm):
    m = x.shape[0]
    row = lambda i: (i, 0)
    return pl.pallas_call(
        _out_proj_kernel,
        grid=(m // tm,),
        in_specs=[pl.BlockSpec((tm, D_MODEL), row),
                  pl.BlockSpec((tm, ATTN_WIDTH), row),
                  pl.BlockSpec((tm, SSM_WIDTH), row),
                  _of_layer(l, (D_MODEL, D_MODEL))],
        out_specs=pl.BlockSpec((tm, D_MODEL), row),
        out_shape=jax.ShapeDtypeStruct((m, D_MODEL), F32),
        compiler_params=_params(("arbitrary",), 48),
        name="out_proj",
    )(x, a, s, w)


FF_TILE = 512
N_FF = D_FF // FF_TILE
FF_SUB = MXU_DIM
CONV_PAD = SUBLANES


def _ffn_conv_specs(l, idx):
    gcol = lambda *g: (l, 0, idx(*g))
    vcol = lambda *g: (l, 0, N_FF + idx(*g))
    return [pl.BlockSpec((None, CONV_W, FF_TILE), gcol), pl.BlockSpec((None, CONV_W, FF_TILE), vcol),
            pl.BlockSpec((None, 1, FF_TILE), gcol), pl.BlockSpec((None, 1, FF_TILE), vcol)]


def _ffn_cast_weight_specs(idx):
    col = lambda *g: (0, idx(*g))
    return [pl.BlockSpec((D_MODEL, FF_TILE), col), pl.BlockSpec((D_MODEL, FF_TILE), col),
            pl.BlockSpec((FF_TILE, D_MODEL), lambda *g: (idx(*g), 0))]


def _ffn_prompt_kernel(final_norm, tiles_per_seq,
                       x_ref, gn_ref, fg_ref, wug_ref, wuv_ref, wd_ref, cwg_ref, cwv_ref, cbg_ref, cbv_ref,
                       o_ref, cg_ref, cv_ref,
                       h_s, extg_s, extv_s, carryg_s, carryv_s):
    m = pl.program_id(0)
    f = pl.program_id(1)
    tm = x_ref.shape[0]

    @pl.when(f == 0)
    def _():
        x = x_ref[...]
        h_s[...] = _rms(x, gn_ref[...]).astype(BF16)
        o_ref[...] = x

    seq_start = (m % tiles_per_seq) == 0

    @pl.when(seq_start)
    def _():
        extg_s[0:CONV_PAD, :] = jnp.zeros((CONV_PAD, FF_TILE), F32)
        extv_s[0:CONV_PAD, :] = jnp.zeros((CONV_PAD, FF_TILE), F32)

    @pl.when(jnp.logical_not(seq_start))
    def _():
        extg_s[0:CONV_PAD, :] = carryg_s[f]
        extv_s[0:CONV_PAD, :] = carryv_s[f]

    def conv(ext_s, cw_ref, cb_ref, cs):
        out = cb_ref[:, cs]
        for t in range(CONV_W):
            lag = CONV_W - 1 - t
            out = out + cw_ref[t:t + 1, cs] * ext_s[CONV_PAD - lag:CONV_PAD - lag + tm, cs]
        return out

    subs = [slice(s * FF_SUB, (s + 1) * FF_SUB) for s in range(FF_TILE // FF_SUB)]
    for cs in subs:
        extg_s[CONV_PAD:, cs] = jnp.dot(h_s[...], wug_ref[:, cs], preferred_element_type=F32)
    for cs in subs:
        extv_s[CONV_PAD:, cs] = jnp.dot(h_s[...], wuv_ref[:, cs], preferred_element_type=F32)
    gates = [jax.nn.silu(conv(extg_s, cwg_ref, cbg_ref, cs)) for cs in subs]
    down = None
    for gate, cs in zip(gates, subs):
        act = (gate * conv(extv_s, cwv_ref, cbv_ref, cs)).astype(BF16)
        part = jnp.dot(act, wd_ref[cs, :], preferred_element_type=F32)
        down = part if down is None else down + part
    o_ref[...] += down

    carryg_s[f] = extg_s[tm:tm + CONV_PAD, :]
    carryv_s[f] = extv_s[tm:tm + CONV_PAD, :]
    cg_ref[f] = extg_s[tm + CONV_PAD - (CONV_W - 1):tm + CONV_PAD, :]
    cv_ref[f] = extv_s[tm + CONV_PAD - (CONV_W - 1):tm + CONV_PAD, :]

    if final_norm:
        @pl.when(f == pl.num_programs(1) - 1)
        def _():
            o_ref[...] = _rms(o_ref[...], fg_ref[...])


def _ffn_prompt(l, x, gn, wug, wuv, wd, cw, cb, fg, batch, seq, tm, final_norm):
    m = batch * seq
    tiles_per_seq = seq // tm
    row = lambda i, f: (i, 0)
    state = lambda i, f: (i // tiles_per_seq, 0, 0, 0)
    return pl.pallas_call(
        functools.partial(_ffn_prompt_kernel, final_norm, tiles_per_seq),
        grid=(m // tm, N_FF),
        in_specs=[pl.BlockSpec((tm, D_MODEL), row, pipeline_mode=pl.Buffered(1)),
                  _of_layer(l, (1, D_MODEL)), _resident((1, D_MODEL))]
                 + _ffn_cast_weight_specs(lambda i, f: f) + _ffn_conv_specs(l, lambda i, f: f),
        out_specs=[pl.BlockSpec((tm, D_MODEL), row),
                   pl.BlockSpec((None, N_FF, CONV_W - 1, FF_TILE), state),
                   pl.BlockSpec((None, N_FF, CONV_W - 1, FF_TILE), state)],
        out_shape=[jax.ShapeDtypeStruct((m, D_MODEL), F32),
                   jax.ShapeDtypeStruct((batch, N_FF, CONV_W - 1, FF_TILE), F32),
                   jax.ShapeDtypeStruct((batch, N_FF, CONV_W - 1, FF_TILE), F32)],
        scratch_shapes=[pltpu.VMEM((tm, D_MODEL), BF16),
                        pltpu.VMEM((tm + CONV_PAD, FF_TILE), F32),
                        pltpu.VMEM((tm + CONV_PAD, FF_TILE), F32),
                        pltpu.VMEM((N_FF, CONV_PAD, FF_TILE), F32),
                        pltpu.VMEM((N_FF, CONV_PAD, FF_TILE), F32)],
        compiler_params=_params(("arbitrary", "arbitrary"), 56),
        name="ffn_prompt",
    )(x, gn, fg, wug, wuv, wd, cw, cw, cb, cb)


CAST_ROWS = 256


def _cast_weight(w_ref, wb_ref):
    for r in range(0, w_ref.shape[0], CAST_ROWS):
        wb_ref[r:r + CAST_ROWS, :] = w_ref[r:r + CAST_ROWS, :].astype(BF16)


def _ffn_sample_kernel(final_norm,
                       x_ref, gn_ref, fg_ref, wug_ref, wuv_ref, wd_ref, cwg_ref, cwv_ref, cbg_ref, cbv_ref,
                       s0g_ref, s0v_ref, s1g_ref, s1v_ref,
                       o_ref, ug_ref, uv_ref, wugb_ref, wuvb_ref, wdb_ref, h_s):
    f = pl.program_id(0)

    @pl.when(f == 0)
    def _():
        x = x_ref[...]
        h_s[...] = _rms(x, gn_ref[...]).astype(BF16)
        o_ref[...] = x

    def conv(w_ref, wb_ref, cw_ref, cb_ref, s0_ref, s1_ref, up_ref):
        _cast_weight(w_ref, wb_ref)
        up = jnp.dot(h_s[...], wb_ref[...], preferred_element_type=F32)
        up_ref[...] = up
        return ((cb_ref[...] + cw_ref[0:1, :] * s0_ref[...]) + cw_ref[1:2, :] * s1_ref[...]) + cw_ref[2:3, :] * up

    gate = conv(wug_ref, wugb_ref, cwg_ref, cbg_ref, s0g_ref, s1g_ref, ug_ref)
    val = conv(wuv_ref, wuvb_ref, cwv_ref, cbv_ref, s0v_ref, s1v_ref, uv_ref)
    act = (jax.nn.silu(gate) * val).astype(BF16)
    _cast_weight(wd_ref, wdb_ref)
    o_ref[...] += jnp.dot(act, wdb_ref[...], preferred_element_type=F32)

    if final_norm:
        @pl.when(f == pl.num_programs(0) - 1)
        def _():
            o_ref[...] = _rms(o_ref[...], fg_ref[...])


def _ffn_sample(l, x, gn, wu, wd, cw, cb, fg, s0, s1, final_norm):
    n = x.shape[0]
    gcol = lambda f: (l, 0, f)
    vcol = lambda f: (l, 0, N_FF + f)
    st = lambda im: pl.BlockSpec((None, n, FF_TILE), im)
    return pl.pallas_call(
        functools.partial(_ffn_sample_kernel, final_norm),
        grid=(N_FF,),
        in_specs=[_resident((n, D_MODEL)), _of_layer(l, (1, D_MODEL)), _resident((1, D_MODEL)),
                  pl.BlockSpec((None, D_MODEL, FF_TILE), gcol), pl.BlockSpec((None, D_MODEL, FF_TILE), vcol),
                  pl.BlockSpec((None, FF_TILE, D_MODEL), lambda f: (l, f, 0))]
                 + _ffn_conv_specs(l, lambda f: f)
                 + [st(gcol), st(vcol), st(gcol), st(vcol)],
        out_specs=[pl.BlockSpec((n, D_MODEL), lambda f: (0, 0)),
                   pl.BlockSpec((n, FF_TILE), lambda f: (0, f)),
                   pl.BlockSpec((n, FF_TILE), lambda f: (0, f))]
                  + _ffn_cast_weight_specs(lambda f: f),
        out_shape=[jax.ShapeDtypeStruct((n, D_MODEL), F32),
                   jax.ShapeDtypeStruct((n, D_FF), F32),
                   jax.ShapeDtypeStruct((n, D_FF), F32),
                   jax.ShapeDtypeStruct((D_MODEL, D_FF), BF16),
                   jax.ShapeDtypeStruct((D_MODEL, D_FF), BF16),
                   jax.ShapeDtypeStruct((D_FF, D_MODEL), BF16)],
        scratch_shapes=[pltpu.VMEM((n, D_MODEL), BF16)],
        compiler_params=_params(("arbitrary",), 44),
        name="ffn_sample",
    )(x, gn, fg, wu, wu, wd, cw, cw, cb, cb, s0, s0, s1, s1)


def _rope_tables(pos):
    half = HEAD_DIM // 2
    inv = ROPE_THETA ** (-jnp.arange(half, dtype=F32) / half)
    ang = pos.astype(F32)[:, None] * inv[None, :]
    cos, sin = jnp.cos(ang), jnp.sin(ang)
    reps = LANES // HEAD_DIM
    return (jnp.concatenate([cos, cos] * reps, axis=1),
            jnp.concatenate([-sin, sin] * reps, axis=1))


def _ssm_params(a_re, a_im, b_re, b_im, c_re, c_im, d, log_dt):
    depth = a_re.shape[0]
    dt = jnp.exp(log_dt)[..., None]
    mag = jnp.exp(a_re * dt)
    lr, li = mag * jnp.cos(a_im * dt), mag * jnp.sin(a_im * dt)
    nr, ni = lr - 1.0, li
    den = a_re * a_re + a_im * a_im
    qr = (nr * a_re + ni * a_im) / den
    qi = (ni * a_re - nr * a_im) / den
    bbr = qr[..., None] * b_re - qi[..., None] * b_im
    bbi = qr[..., None] * b_im + qi[..., None] * b_re

    def compact(t):
        r, c = t.shape[2], t.shape[3]
        t = t.reshape(depth, N_CHUNK, GROUP_CHUNK * r, c).astype(BF16)
        return jnp.pad(t, ((0, 0), (0, 0), (0, 0), (0, LANES - c)))

    def lane_tiling(c):
        src = jnp.arange(LANES)[:, None]
        dst = jnp.arange(GROUP_CHUNK * c)[None, :] % c
        return (src == dst).astype(BF16)

    return dict(bre=compact(bbr.transpose(0, 1, 3, 2)), bim=compact(bbi.transpose(0, 1, 3, 2)),
                cre=compact(c_re.transpose(0, 1, 3, 2)), cimn=compact((-c_im).transpose(0, 1, 3, 2)),
                tile_b=lane_tiling(SSM_STATE), tile_c=lane_tiling(SSM_GROUP),
                lre=lr.reshape(depth, 1, N_STATE), lim=li.reshape(depth, 1, N_STATE),
                d=d.reshape(depth, 1, SSM_WIDTH))


def _segment_permutation():
    r = jnp.arange(SSM_TILE)
    src = (r % SUBLANES) * SEG_LEN + r // SUBLANES
    return (src[:, None] == jnp.arange(SSM_TILE)[None, :]).astype(BF16)


def kernel(x_prompt, x_sample, cache_k, cache_v, state_ssm_re, state_ssm_im, state_conv, attn_norm_g, w_in, attn_sinks, ssm_a_re, ssm_a_im, ssm_b_re, ssm_b_im, ssm_c_re, ssm_c_im, ssm_d, ssm_log_dt, w_glu, b_glu, attn_out_norm_g, ssm_out_norm_g, w_out, ffn_norm_g, w_up, conv_w, conv_b, w_down, final_norm_g):
    batch, seq, _ = x_prompt.shape
    nsamp, dec_seq, _ = x_sample.shape
    wbuf = cache_k.shape[2]
    assert dec_seq == 1 and wbuf == WINDOW and seq % (SSM_SUBTILES * SSM_TILE) == 0
    assert PAST_LEN >= wbuf

    w_in_b, w_glu_b, w_out_b = w_in.astype(BF16), w_glu.astype(BF16), w_out.astype(BF16)
    ssm = _ssm_params(ssm_a_re, ssm_a_im, ssm_b_re, ssm_b_im, ssm_c_re, ssm_c_im, ssm_d, ssm_log_dt)
    perm = _segment_permutation()

    cos_p, sin_p = _rope_tables(jnp.arange(seq, dtype=jnp.int32))
    cos_s, sin_s = _rope_tables(jnp.full((nsamp,), PAST_LEN, dtype=jnp.int32))

    rows = lambda a: a.reshape(a.shape[0], 1, a.shape[-1])
    g_in, g_a, g_s, g_f = rows(attn_norm_g), rows(attn_out_norm_g), rows(ssm_out_norm_g), rows(ffn_norm_g)
    bg, cb = rows(b_glu), rows(conv_b)
    fg = final_norm_g.reshape(1, D_MODEL)
    sink_col = attn_sinks.reshape(DEPTH, N_HEADS, 1)
    head_is_lo = (jnp.arange(N_HEADS) < KV_REP)[None, :, None]
    kc = cache_k.reshape(DEPTH, nsamp, wbuf, KV_COLS)
    vc = cache_v.reshape(DEPTH, nsamp, wbuf, KV_COLS)
    h0re = state_ssm_re.reshape(DEPTH, nsamp, N_STATE)
    h0im = state_ssm_im.reshape(DEPTH, nsamp, N_STATE)
    conv0, conv1 = state_conv[:, :, 0, :], state_conv[:, :, 1, :]

    xp = x_prompt.reshape(batch * seq, D_MODEL)
    xs = x_sample.reshape(nsamp, D_MODEL)
    tm_p = 512
    tm_ffn = 1024
    outs = {k: [] for k in ("kp", "vp", "hrp", "hip", "cgp", "cvp", "ks", "vs", "hrs", "his", "ugs", "uvs")}

    for l in range(DEPTH):
        last = l == DEPTH - 1

        q, k, v, u = _in_proj(l, xs, g_in, w_in_b, cos_s, sin_s, nsamp)
        qh = q.reshape(nsamp, N_HEADS, HEAD_DIM)
        zq = jnp.zeros_like(qh)
        qbd = jnp.where(head_is_lo, jnp.concatenate([qh, zq], axis=-1), jnp.concatenate([zq, qh], axis=-1))
        o = _attn_sample(l, qbd, kc, vc, k.reshape(nsamp, 1, KV_COLS), v.reshape(nsamp, 1, KV_COLS), sink_col, 8)
        a_raw = jnp.concatenate([o[:, :KV_REP, :HEAD_DIM].reshape(nsamp, -1),
                                 o[:, KV_REP:, HEAD_DIM:].reshape(nsamp, -1)], axis=1)
        a = _rms_cast(l, a_raw, g_a)
        s, hre, him = _ssm_sample(l, u, h0re, h0im, ssm, w_glu_b, bg, g_s)
        x1 = _out_proj(l, xs, a, s, w_out_b, nsamp)
        xs, ug, uv, wug_b, wuv_b, wd_b = _ffn_sample(l, x1, g_f, w_up, w_down, conv_w, cb, fg, conv0, conv1, last)
        outs["ks"].append(k)
        outs["vs"].append(v)
        outs["hrs"].append(hre)
        outs["his"].append(him)
        outs["ugs"].append(ug)
        outs["uvs"].append(uv)

        q, k, v, u = _in_proj(l, xp, g_in, w_in_b, cos_p, sin_p, tm_p)
        a = _attn_prompt(l, q, k, v, attn_sinks, g_a, batch, seq)
        s, hre, him = _ssm_prompt(l, u, perm, ssm, w_glu_b, bg, g_s, batch, seq)
        x1 = _out_proj(l, xp, a, s, w_out_b, tm_p)
        xp, cg, cv = _ffn_prompt(l, x1, g_f, wug_b, wuv_b, wd_b, conv_w, cb, fg, batch, seq, tm_ffn, last)
        outs["kp"].append(k.reshape(batch, seq, KV_COLS)[:, seq - WINDOW:])
        outs["vp"].append(v.reshape(batch, seq, KV_COLS)[:, seq - WINDOW:])
        outs["hrp"].append(hre)
        outs["hip"].append(him)
        outs["cgp"].append(cg)
        outs["cvp"].append(cv)

    st = lambda name: jnp.stack(outs[name], axis=0)
    heads = lambda t: t.reshape(t.shape[:-1] + (N_KV_HEADS, HEAD_DIM))
    states = lambda t: t.reshape(DEPTH, -1, SSM_GROUPS, SSM_STATE)
    unblock = lambda t: t.transpose(0, 1, 3, 2, 4).reshape(DEPTH, batch, CONV_W - 1, D_FF)
    conv_prompt = jnp.concatenate([unblock(st("cgp")), unblock(st("cvp"))], axis=-1)
    k_sample = jnp.concatenate([cache_k[:, :, 1:], heads(st("ks"))[:, :, None]], axis=2)
    v_sample = jnp.concatenate([cache_v[:, :, 1:], heads(st("vs"))[:, :, None]], axis=2)
    conv_sample = jnp.stack([conv1, jnp.concatenate([st("ugs"), st("uvs")], axis=-1)], axis=2)
    return (xp.reshape(batch, seq, D_MODEL), xs.reshape(nsamp, 1, D_MODEL),
            heads(st("kp")), heads(st("vp")), states(st("hrp")), states(st("hip")), conv_prompt,
            k_sample, v_sample, states(st("hrs")), states(st("his")), conv_sample)
```

```python
import functools

import jax
import jax.numpy as jnp
from jax import lax
from jax.experimental import pallas as pl
from jax.experimental.pallas import tpu as pltpu

F32 = jnp.float32
BF16 = jnp.bfloat16

D_MODEL = 2048
DEPTH = 4
ATTN_WIDTH = 1024
SSM_WIDTH = 1024
HEAD_DIM = 64
N_HEADS = 16
N_KV_HEADS = 2
KV_REP = 8
KV_COLS = N_KV_HEADS * HEAD_DIM
WINDOW = 128
ROPE_THETA = 10000.0
SSM_GROUP = 16
SSM_GROUPS = 64
SSM_STATE = 64
N_STATE = SSM_GROUPS * SSM_STATE
D_FF = 5632
CONV_W = 3
RMS_EPS = 1e-6
IN_COLS = ATTN_WIDTH + 2 * KV_COLS + SSM_WIDTH
NEG_INF = -1e30
PAST_LEN = 16384
SCORE_SCALE = HEAD_DIM ** -0.5
assert SCORE_SCALE == 2.0 ** -3

LANES = 128
SUBLANES = 8
MXU_DIM = 256
MIB = 1024 * 1024
U_BLOCKS = SSM_WIDTH // LANES

GROUP_CHUNK = 16
N_CHUNK = SSM_GROUPS // GROUP_CHUNK
CH_U = GROUP_CHUNK * SSM_GROUP
CH_S = GROUP_CHUNK * SSM_STATE
BLOCKS_PER_CHUNK = CH_U // LANES

SEG_LEN = 32
SSM_TILE = SUBLANES * SEG_LEN
SSM_SUBTILES = 2
SCAN_LANES = 512


def _rms(x, g):
    ms = jnp.mean(x * x, axis=-1, keepdims=True)
    return x * lax.rsqrt(ms + RMS_EPS) * g


def _params(sem, vmem_mib):
    return pltpu.CompilerParams(dimension_semantics=sem, vmem_limit_bytes=vmem_mib * MIB)


def _resident(shape):
    nd = len(shape)
    return pl.BlockSpec(shape, lambda *_: (0,) * nd, pipeline_mode=pl.Buffered(1))


def _of_layer(l, shape):
    nd = len(shape)
    return pl.BlockSpec((None,) + tuple(shape), lambda *_: (l,) + (0,) * nd, pipeline_mode=pl.Buffered(1))


def _in_proj_kernel(x_ref, g_ref, w_ref, cos_ref, sin_ref, q_ref, k_ref, v_ref, u_ref):
    h = _rms(x_ref[...], g_ref[...]).astype(BF16)
    proj = jnp.dot(h, w_ref[...], preferred_element_type=F32)
    cos = cos_ref[...]
    sin = sin_ref[...]
    lane = lax.broadcasted_iota(jnp.int32, cos.shape, 1)
    first_half = (lane & (HEAD_DIM - 1)) < (HEAD_DIM // 2)

    def rope(blk):
        partner = jnp.where(first_half,
                            pltpu.roll(blk, LANES - HEAD_DIM // 2, 1),
                            pltpu.roll(blk, HEAD_DIM // 2, 1))
        return blk * cos + partner * sin

    for j in range(ATTN_WIDTH // LANES):
        q_ref[:, j * LANES:(j + 1) * LANES] = (rope(proj[:, j * LANES:(j + 1) * LANES]) * SCORE_SCALE).astype(BF16)
    k_ref[...] = rope(proj[:, ATTN_WIDTH:ATTN_WIDTH + KV_COLS])
    v_ref[...] = proj[:, ATTN_WIDTH + KV_COLS:ATTN_WIDTH + 2 * KV_COLS]
    u0 = ATTN_WIDTH + 2 * KV_COLS
    for j in range(U_BLOCKS):
        u_ref[j] = proj[:, u0 + j * LANES:u0 + (j + 1) * LANES]


def _in_proj(l, x, g, w, cos, sin, tm):
    m = x.shape[0]
    pos_tiles = cos.shape[0] // tm
    row = lambda i: (i, 0)
    return pl.pallas_call(
        _in_proj_kernel,
        grid=(m // tm,),
        in_specs=[pl.BlockSpec((tm, D_MODEL), row),
                  _of_layer(l, (1, D_MODEL)),
                  _of_layer(l, (D_MODEL, IN_COLS)),
                  pl.BlockSpec((tm, LANES), lambda i: (i % pos_tiles, 0)),
                  pl.BlockSpec((tm, LANES), lambda i: (i % pos_tiles, 0))],
        out_specs=[pl.BlockSpec((tm, ATTN_WIDTH), row),
                   pl.BlockSpec((tm, KV_COLS), row),
                   pl.BlockSpec((tm, KV_COLS), row),
                   pl.BlockSpec((U_BLOCKS, tm, LANES), lambda i: (0, i, 0))],
        out_shape=[jax.ShapeDtypeStruct((m, ATTN_WIDTH), BF16),
                   jax.ShapeDtypeStruct((m, KV_COLS), F32),
                   jax.ShapeDtypeStruct((m, KV_COLS), F32),
                   jax.ShapeDtypeStruct((U_BLOCKS, m, LANES), F32)],
        compiler_params=_params(("arbitrary",), 48),
        name="in_proj",
    )(x, g, w, cos, sin)


ATTN_SUB = 4


def _attn_prompt_kernel(layer, sinks_ref, q_ref, kc_ref, kp_ref, vc_ref, vp_ref, gn_ref, a_ref):
    j = pl.program_id(1)
    lane = lax.broadcasted_iota(jnp.int32, (WINDOW, LANES), 1)
    lo = lane < HEAD_DIM
    qi = lax.broadcasted_iota(jnp.int32, (WINDOW, 2 * WINDOW), 0)
    kj = lax.broadcasted_iota(jnp.int32, (WINDOW, 2 * WINDOW), 1)
    diff = qi + WINDOW - kj
    band = (diff >= 0) & (diff <= WINDOW)
    first_key = jnp.where(j > 0, 0, WINDOW)
    valid = [band & (kj >= first_key)] + [band] * (ATTN_SUB - 1)
    zero = jnp.zeros((WINDOW, LANES), F32)

    def halves(x, g):
        r = pltpu.roll(x, HEAD_DIM, 1)
        if g == 0:
            return jnp.where(lo, x, zero).astype(BF16), jnp.where(lo, zero, r).astype(BF16)
        return jnp.where(lo, r, zero).astype(BF16), jnp.where(lo, zero, x).astype(BF16)

    rows = lambda sb: slice(sb * WINDOW, (sb + 1) * WINDOW)
    k_blocks = [kp_ref[...]] + [kc_ref[rows(sb), :] for sb in range(ATTN_SUB)]
    v_blocks = [vp_ref[...]] + [vc_ref[rows(sb), :] for sb in range(ATTN_SUB)]
    k_halves = [[halves(x, g) for x in k_blocks] for g in range(N_KV_HEADS)]
    v_halves = [[halves(x, g) for x in v_blocks] for g in range(N_KV_HEADS)]
    pairs = KV_REP // 2
    nt = (((1,), (1,)), ((), ()))

    for sb in range(ATTN_SUB):
        def softmax(s, head, sb=sb):
            s = jnp.where(valid[sb], s, NEG_INF)
            sink = sinks_ref[layer, head]
            m = jnp.maximum(jnp.max(s, axis=-1, keepdims=True), sink)
            p = jnp.exp(s - m)
            denom = jnp.sum(p, axis=-1, keepdims=True) + jnp.exp(sink - m)
            return (p / denom).astype(BF16)

        out_blocks = []
        for g in range(N_KV_HEADS):
            window = lambda hv, half: jnp.concatenate([hv[g][sb][half], hv[g][sb + 1][half]], axis=0)
            k_lo, k_hi, v_lo, v_hi = window(k_halves, 0), window(k_halves, 1), window(v_halves, 0), window(v_halves, 1)
            qg = jnp.concatenate([q_ref[rows(sb), (g * pairs + p) * LANES:(g * pairs + p + 1) * LANES]
                                  for p in range(pairs)], axis=0)
            s_even = lax.dot_general(qg, k_lo, nt, preferred_element_type=F32)
            s_odd = lax.dot_general(qg, k_hi, nt, preferred_element_type=F32)
            p_even = jnp.concatenate(
                [softmax(s_even[rows(p)], g * KV_REP + 2 * p) for p in range(pairs)], axis=0)
            p_odd = jnp.concatenate(
                [softmax(s_odd[rows(p)], g * KV_REP + 2 * p + 1) for p in range(pairs)], axis=0)
            o = (jnp.dot(p_even, v_lo, preferred_element_type=F32)
                 + jnp.dot(p_odd, v_hi, preferred_element_type=F32))
            out_blocks += [o[rows(p)] for p in range(pairs)]

        ssq = out_blocks[0] * out_blocks[0]
        for blk in out_blocks[1:]:
            ssq = ssq + blk * blk
        inv = lax.rsqrt(jnp.sum(ssq, axis=-1, keepdims=True) / ATTN_WIDTH + RMS_EPS)
        for i, blk in enumerate(out_blocks):
            cols = slice(i * LANES, (i + 1) * LANES)
            a_ref[rows(sb), cols] = (blk * inv * gn_ref[:, cols]).astype(BF16)


def _attn_prompt(l, q, k, v, sinks, gn, batch, seq):
    nb = seq // WINDOW
    steps = nb // ATTN_SUB
    cur = lambda b, j: (b * steps + j, 0)
    prev = lambda b, j: (b * nb + jnp.maximum(j * ATTN_SUB - 1, 0), 0)
    return pl.pallas_call(
        functools.partial(_attn_prompt_kernel, l),
        grid=(batch, steps),
        in_specs=[pl.BlockSpec(memory_space=pltpu.SMEM),
                  pl.BlockSpec((ATTN_SUB * WINDOW, ATTN_WIDTH), cur),
                  pl.BlockSpec((ATTN_SUB * WINDOW, KV_COLS), cur), pl.BlockSpec((WINDOW, KV_COLS), prev),
                  pl.BlockSpec((ATTN_SUB * WINDOW, KV_COLS), cur), pl.BlockSpec((WINDOW, KV_COLS), prev),
                  _of_layer(l, (1, ATTN_WIDTH))],
        out_specs=pl.BlockSpec((ATTN_SUB * WINDOW, ATTN_WIDTH), cur),
        out_shape=jax.ShapeDtypeStruct((batch * seq, ATTN_WIDTH), BF16),
        compiler_params=_params(("arbitrary", "arbitrary"), 32),
        name="attn_prompt",
    )(sinks, q, k, k, v, v, gn)


def _attn_sample_kernel(qbd_ref, kc_ref, vc_ref, kn_ref, vn_ref, sink_ref, o_ref):
    bb = qbd_ref.shape[0]
    rows, keys = bb * N_HEADS, bb * WINDOW
    q = qbd_ref[...].reshape(rows, LANES)
    kc = kc_ref[...].reshape(keys, KV_COLS).astype(BF16)
    vc = vc_ref[...].reshape(keys, KV_COLS).astype(BF16)
    per_head = lambda ref: jnp.broadcast_to(ref[...], (bb, N_HEADS, KV_COLS)).reshape(rows, KV_COLS)
    kn, vn = per_head(kn_ref), per_head(vn_ref)
    sink = jnp.concatenate([sink_ref[...]] * bb, axis=0)
    nt = (((1,), (1,)), ((), ()))
    s = lax.dot_general(q, kc, nt, preferred_element_type=F32)
    own = ((lax.broadcasted_iota(jnp.int32, s.shape, 0) >> _log2(N_HEADS))
           == (lax.broadcasted_iota(jnp.int32, s.shape, 1) >> _log2(WINDOW)))
    s = jnp.where(own, s, NEG_INF)
    s_new = jnp.sum(q.astype(F32) * kn, axis=-1, keepdims=True)
    m = jnp.maximum(jnp.maximum(jnp.max(s, axis=-1, keepdims=True), s_new), sink)
    p = jnp.exp(s - m)
    p_new = jnp.exp(s_new - m)
    denom = jnp.sum(p, axis=-1, keepdims=True) + p_new + jnp.exp(sink - m)
    o = jnp.dot((p / denom).astype(BF16), vc, preferred_element_type=F32) + (p_new / denom) * vn
    o_ref[...] = o.reshape(bb, N_HEADS, LANES)


def _attn_sample(l, qbd, kc, vc, kn, vn, sink_col, bb):
    nbatch = qbd.shape[0]
    blk = lambda *s: pl.BlockSpec((bb,) + s, lambda i: (i, 0, 0))
    cache = pl.BlockSpec((None, bb, WINDOW, KV_COLS), lambda i: (l, i, 0, 0))
    return pl.pallas_call(
        _attn_sample_kernel,
        grid=(nbatch // bb,),
        in_specs=[blk(N_HEADS, LANES), cache, cache,
                  blk(1, KV_COLS), blk(1, KV_COLS), _of_layer(l, (N_HEADS, 1))],
        out_specs=blk(N_HEADS, LANES),
        out_shape=jax.ShapeDtypeStruct((nbatch, N_HEADS, LANES), F32),
        compiler_params=_params(("arbitrary",), 32),
        name="attn_sample",
    )(qbd, kc, vc, kn, vn, sink_col)


def _rms_cast_kernel(x_ref, g_ref, o_ref):
    o_ref[...] = _rms(x_ref[...], g_ref[...]).astype(BF16)


def _rms_cast(l, x, g):
    n, d = x.shape
    return pl.pallas_call(
        _rms_cast_kernel,
        grid=(1,),
        in_specs=[_resident((n, d)), _of_layer(l, (1, d))],
        out_specs=pl.BlockSpec((n, d), lambda i: (0, 0)),
        out_shape=jax.ShapeDtypeStruct(x.shape, BF16),
        name="rms_cast",
    )(x, g)


def _glu_norm(z, wg_ref, bg_ref, gn_ref):
    gate = jax.nn.sigmoid(jnp.dot(z.astype(BF16), wg_ref[...], preferred_element_type=F32) + bg_ref[...])
    return _rms(z * gate, gn_ref[...])


def _ssm_operands(l, ssm, wg, bg, gn):
    layered = (ssm["bre"], ssm["bim"], ssm["cre"], ssm["cimn"], ssm["lre"], ssm["lim"], ssm["d"], wg, bg, gn)
    shared = (ssm["tile_b"], ssm["tile_c"])
    return (layered + shared,
            [_of_layer(l, a.shape[1:]) for a in layered] + [_resident(a.shape) for a in shared])


def _log2(n):
    assert n & (n - 1) == 0
    return n.bit_length() - 1


def _block_diag_scratch():
    return [pltpu.VMEM((N_CHUNK, CH_U, CH_S), BF16), pltpu.VMEM((N_CHUNK, CH_U, CH_S), BF16),
            pltpu.VMEM((N_CHUNK, CH_S, CH_U), BF16), pltpu.VMEM((N_CHUNK, CH_S, CH_U), BF16)]


def _expand_block_diag(t_ref, tile_ref, out_s, rows_per_group, cols_per_group):
    for c in range(N_CHUNK):
        tiled = jnp.dot(t_ref[c], tile_ref[...], preferred_element_type=F32)
        row_group = lax.broadcasted_iota(jnp.int32, tiled.shape, 0) >> _log2(rows_per_group)
        col_group = lax.broadcasted_iota(jnp.int32, tiled.shape, 1) >> _log2(cols_per_group)
        out_s[c] = jnp.where(row_group == col_group, tiled, 0.0).astype(BF16)


def _expand_projections(tb_re, tb_im, tc_re, tc_imn, tile_b, tile_c, bre_s, bim_s, cre_s, cimn_s):
    _expand_block_diag(tb_re, tile_b, bre_s, SSM_GROUP, SSM_STATE)
    _expand_block_diag(tb_im, tile_b, bim_s, SSM_GROUP, SSM_STATE)
    _expand_block_diag(tc_re, tile_c, cre_s, SSM_STATE, SSM_GROUP)
    _expand_block_diag(tc_imn, tile_c, cimn_s, SSM_STATE, SSM_GROUP)


def _ssm_prompt_kernel(u_ref, perm_ref, tb_re, tb_im, tc_re, tc_imn, lre_ref, lim_ref, d_ref,
                       wg_ref, bg_ref, gn_ref, tile_b, tile_c,
                       s_ref, hre_ref, him_ref,
                       bre_ref, bim_ref, cre_ref, cimn_ref,
                       xre_s, xim_s, hbre_s, hbim_s, pre_s, pim_s, cre_s, cim_s, hinre_s, hinim_s, y_s):
    b = pl.program_id(0)
    j = pl.program_id(1)

    @pl.when((b == 0) & (j == 0))
    def _():
        _expand_projections(tb_re, tb_im, tc_re, tc_imn, tile_b, tile_c, bre_ref, bim_ref, cre_ref, cimn_ref)
        lr, li = lre_ref[...], lim_ref[...]
        pr, pi = lr, li
        for i in range(SEG_LEN):
            if i:
                pr, pi = pr * lr - pi * li, pr * li + pi * lr
            pre_s[i * SUBLANES:(i + 1) * SUBLANES, :] = jnp.broadcast_to(pr, (SUBLANES, N_STATE))
            pim_s[i * SUBLANES:(i + 1) * SUBLANES, :] = jnp.broadcast_to(pi, (SUBLANES, N_STATE))

    @pl.when(j == 0)
    def _():
        cre_s[...] = jnp.zeros_like(cre_s)
        cim_s[...] = jnp.zeros_like(cim_s)

    u_nat = jnp.concatenate([u_ref[jb] for jb in range(U_BLOCKS)], axis=1)
    u_b = u_nat.astype(BF16)
    u_perm = jnp.concatenate(
        [jnp.dot(perm_ref[...], u_b[t * SSM_TILE:(t + 1) * SSM_TILE], preferred_element_type=F32).astype(BF16)
         for t in range(SSM_SUBTILES)], axis=0)

    last = (SEG_LEN - 1) * SUBLANES
    for c in range(N_CHUNK):
        uc = u_perm[:, c * CH_U:(c + 1) * CH_U]
        chunk = slice(c * CH_S, (c + 1) * CH_S)
        xre_s[:, chunk] = jnp.dot(uc, bre_ref[c], preferred_element_type=F32)
        xim_s[:, chunk] = jnp.dot(uc, bim_ref[c], preferred_element_type=F32)

        for t in range(SSM_SUBTILES):
            for lc in range(CH_S // SCAN_LANES):
                sl = slice(c * CH_S + lc * SCAN_LANES, c * CH_S + (lc + 1) * SCAN_LANES)
                lr = jnp.broadcast_to(lre_ref[:, sl], (SUBLANES, SCAN_LANES))
                li = jnp.broadcast_to(lim_ref[:, sl], (SUBLANES, SCAN_LANES))
                r0 = t * SSM_TILE
                hr = xre_s[r0:r0 + SUBLANES, sl]
                hi = xim_s[r0:r0 + SUBLANES, sl]
                for i in range(1, SEG_LEN):
                    rows = slice(r0 + i * SUBLANES, r0 + (i + 1) * SUBLANES)
                    hr, hi = (lr * hr - li * hi) + xre_s[rows, sl], (lr * hi + li * hr) + xim_s[rows, sl]
                    xre_s[rows, sl] = hr
                    xim_s[rows, sl] = hi

        l32r = pre_s[last:last + 1, chunk]
        l32i = pim_s[last:last + 1, chunk]
        cr, ci = cre_s[:, chunk], cim_s[:, chunk]
        for seg in range(SSM_SUBTILES * SUBLANES):
            hinre_s[seg:seg + 1, chunk] = cr
            hinim_s[seg:seg + 1, chunk] = ci
            end_row = (seg // SUBLANES) * SSM_TILE + last + seg % SUBLANES
            er = xre_s[end_row:end_row + 1, chunk]
            ei = xim_s[end_row:end_row + 1, chunk]
            cr, ci = (l32r * cr - l32i * ci) + er, (l32r * ci + l32i * cr) + ei
        cre_s[:, chunk] = cr
        cim_s[:, chunk] = ci

        for t in range(SSM_SUBTILES):
            for lc in range(CH_S // SCAN_LANES):
                sl = slice(c * CH_S + lc * SCAN_LANES, c * CH_S + (lc + 1) * SCAN_LANES)
                hr_in = jnp.concatenate([hinre_s[t * SUBLANES:(t + 1) * SUBLANES, sl]] * 2, axis=0)
                hi_in = jnp.concatenate([hinim_s[t * SUBLANES:(t + 1) * SUBLANES, sl]] * 2, axis=0)
                for i2 in range(SEG_LEN // 2):
                    prow = slice(i2 * 2 * SUBLANES, (i2 + 1) * 2 * SUBLANES)
                    rows = slice(t * SSM_TILE + prow.start, t * SSM_TILE + prow.stop)
                    pr, pi = pre_s[prow, sl], pim_s[prow, sl]
                    hbre_s[rows, sl] = (xre_s[rows, sl] + (pr * hr_in - pi * hi_in)).astype(BF16)
                    hbim_s[rows, sl] = (xim_s[rows, sl] + (pr * hi_in + pi * hr_in)).astype(BF16)

        y = (jnp.dot(hbre_s[:, chunk], cre_ref[c], preferred_element_type=F32)
             + jnp.dot(hbim_s[:, chunk], cimn_ref[c], preferred_element_type=F32))
        for h in range(BLOCKS_PER_CHUNK):
            y_s[c * BLOCKS_PER_CHUNK + h] = y[:, h * LANES:(h + 1) * LANES]

    y_nat = jnp.concatenate(
        [jnp.concatenate([y_s[jb, pl.ds(t * SSM_TILE + i0 * SUBLANES + seg, SUBLANES, stride=SUBLANES), :]
                          for t in range(SSM_SUBTILES)
                          for seg in range(SUBLANES) for i0 in range(0, SEG_LEN, SUBLANES)], axis=0)
         for jb in range(U_BLOCKS)], axis=1)
    z = jax.nn.gelu(y_nat + d_ref[...] * u_nat)
    s_ref[...] = _glu_norm(z, wg_ref, bg_ref, gn_ref).astype(BF16)

    @pl.when(j == pl.num_programs(1) - 1)
    def _():
        hre_ref[...] = cre_s[...]
        him_ref[...] = cim_s[...]


def _ssm_prompt(l, u, perm, ssm, wg, bg, gn, batch, seq):
    step_rows = SSM_SUBTILES * SSM_TILE
    nt = seq // step_rows
    row = lambda b, j: (b * nt + j, 0)
    st = lambda b, j: (b, 0, 0)
    arrays, specs = _ssm_operands(l, ssm, wg, bg, gn)
    scratch = _block_diag_scratch() + [
        pltpu.VMEM((step_rows, N_STATE), F32),
        pltpu.VMEM((step_rows, N_STATE), F32),
        pltpu.VMEM((step_rows, N_STATE), BF16),
        pltpu.VMEM((step_rows, N_STATE), BF16),
        pltpu.VMEM((SSM_TILE, N_STATE), F32),
        pltpu.VMEM((SSM_TILE, N_STATE), F32),
        pltpu.VMEM((1, N_STATE), F32),
        pltpu.VMEM((1, N_STATE), F32),
        pltpu.VMEM((SSM_SUBTILES * SUBLANES, N_STATE), F32),
        pltpu.VMEM((SSM_SUBTILES * SUBLANES, N_STATE), F32),
        pltpu.VMEM((U_BLOCKS, step_rows, LANES), F32),
    ]
    return pl.pallas_call(
        _ssm_prompt_kernel,
        grid=(batch, nt),
        in_specs=[pl.BlockSpec((U_BLOCKS, step_rows, LANES), lambda b, j: (0, b * nt + j, 0)),
                  _resident((SSM_TILE, SSM_TILE))] + specs,
        out_specs=[pl.BlockSpec((step_rows, SSM_WIDTH), row),
                   pl.BlockSpec((None, 1, N_STATE), st),
                   pl.BlockSpec((None, 1, N_STATE), st)],
        out_shape=[jax.ShapeDtypeStruct((batch * seq, SSM_WIDTH), BF16),
                   jax.ShapeDtypeStruct((batch, 1, N_STATE), F32),
                   jax.ShapeDtypeStruct((batch, 1, N_STATE), F32)],
        scratch_shapes=scratch,
        compiler_params=_params(("arbitrary", "arbitrary"), 56),
        name="ssm_prompt",
    )(u, perm, *arrays)


def _ssm_sample_kernel(u_ref, h0re_ref, h0im_ref, tb_re, tb_im, tc_re, tc_imn, lre_ref, lim_ref, d_ref,
                       wg_ref, bg_ref, gn_ref, tile_b, tile_c, s_ref, hre_ref, him_ref,
                       bre_ref, bim_ref, cre_ref, cimn_ref):
    _expand_projections(tb_re, tb_im, tc_re, tc_imn, tile_b, tile_c, bre_ref, bim_ref, cre_ref, cimn_ref)
    u = jnp.concatenate([u_ref[jb] for jb in range(U_BLOCKS)], axis=1)
    zs = []
    for c in range(N_CHUNK):
        cs = slice(c * CH_U, (c + 1) * CH_U)
        ss = slice(c * CH_S, (c + 1) * CH_S)
        uc = u[:, cs].astype(BF16)
        lr, li = lre_ref[:, ss], lim_ref[:, ss]
        h0r, h0i = h0re_ref[:, ss], h0im_ref[:, ss]
        hr = jnp.dot(uc, bre_ref[c], preferred_element_type=F32) + (lr * h0r - li * h0i)
        hi = jnp.dot(uc, bim_ref[c], preferred_element_type=F32) + (lr * h0i + li * h0r)
        hre_ref[:, ss] = hr
        him_ref[:, ss] = hi
        y = (jnp.dot(hr.astype(BF16), cre_ref[c], preferred_element_type=F32)
             + jnp.dot(hi.astype(BF16), cimn_ref[c], preferred_element_type=F32))
        zs.append(jax.nn.gelu(y + d_ref[:, cs] * u[:, cs]))
    z = jnp.concatenate(zs, axis=1)
    s_ref[...] = _glu_norm(z, wg_ref, bg_ref, gn_ref).astype(BF16)


def _ssm_sample(l, u, h0re, h0im, ssm, wg, bg, gn):
    n = u.shape[1]
    arrays, specs = _ssm_operands(l, ssm, wg, bg, gn)
    whole = lambda *s: pl.BlockSpec(s, lambda i: (0,) * len(s))
    return pl.pallas_call(
        _ssm_sample_kernel,
        grid=(1,),
        in_specs=[_resident((U_BLOCKS, n, LANES)), _of_layer(l, (n, N_STATE)), _of_layer(l, (n, N_STATE))] + specs,
        out_specs=[whole(n, SSM_WIDTH), whole(n, N_STATE), whole(n, N_STATE)],
        out_shape=[jax.ShapeDtypeStruct((n, SSM_WIDTH), BF16),
                   jax.ShapeDtypeStruct((n, N_STATE), F32),
                   jax.ShapeDtypeStruct((n, N_STATE), F32)],
        scratch_shapes=_block_diag_scratch(),
        compiler_params=_params(("arbitrary",), 40),
        name="ssm_sample",
    )(u, h0re, h0im, *arrays)


def _out_proj_kernel(x_ref, a_ref, s_ref, w_ref, o_ref):
    acc = (jnp.dot(a_ref[...], w_ref[0:ATTN_WIDTH, :], preferred_element_type=F32)
           + jnp.dot(s_ref[...], w_ref[ATTN_WIDTH:, :], preferred_element_type=F32))
    o_ref[...] = x_ref[...] + acc


def _out_proj(l, x, a, s, w, tm):
    m = x.shape[0]
    row = lambda i: (i, 0)
    return pl.pallas_call(
        _out_proj_kernel,
        grid=(m // tm,),
        in_specs=[pl.BlockSpec((tm, D_MODEL), row),
                  pl.BlockSpec((tm, ATTN_WIDTH), row),
                  pl.BlockSpec((tm, SSM_WIDTH), row),
                  _of_layer(l, (D_MODEL, D_MODEL))],
        out_specs=pl.BlockSpec((tm, D_MODEL), row),
        out_shape=jax.ShapeDtypeStruct((m, D_MODEL), F32),
        compiler_params=_params(("arbitrary",), 48),
        name="out_proj",
    )(x, a, s, w)


FF_TILE = 512
N_FF = D_FF // FF_TILE
FF_SUB = MXU_DIM
CONV_PAD = SUBLANES


def _ffn_conv_specs(l, idx):
    gcol = lambda *g: (l, 0, idx(*g))
    vcol = lambda *g: (l, 0, N_FF + idx(*g))
    return [pl.BlockSpec((None, CONV_W, FF_TILE), gcol), pl.BlockSpec((None, CONV_W, FF_TILE), vcol),
            pl.BlockSpec((None, 1, FF_TILE), gcol), pl.BlockSpec((None, 1, FF_TILE), vcol)]


def _ffn_cast_weight_specs(idx):
    col = lambda *g: (0, idx(*g))
    return [pl.BlockSpec((D_MODEL, FF_TILE), col), pl.BlockSpec((D_MODEL, FF_TILE), col),
            pl.BlockSpec((FF_TILE, D_MODEL), lambda *g: (idx(*g), 0))]


def _ffn_prompt_kernel(final_norm, tiles_per_seq,
                       x_ref, gn_ref, fg_ref, wug_ref, wuv_ref, wd_ref, cwg_ref, cwv_ref, cbg_ref, cbv_ref,
                       o_ref, cg_ref, cv_ref,
                       h_s, extg_s, extv_s, carryg_s, carryv_s):
    m = pl.program_id(0)
    f = pl.program_id(1)
    tm = x_ref.shape[0]

    @pl.when(f == 0)
    def _():
        x = x_ref[...]
        h_s[...] = _rms(x, gn_ref[...]).astype(BF16)
        o_ref[...] = x

    seq_start = (m % tiles_per_seq) == 0

    @pl.when(seq_start)
    def _():
        extg_s[0:CONV_PAD, :] = jnp.zeros((CONV_PAD, FF_TILE), F32)
        extv_s[0:CONV_PAD, :] = jnp.zeros((CONV_PAD, FF_TILE), F32)

    @pl.when(jnp.logical_not(seq_start))
    def _():
        extg_s[0:CONV_PAD, :] = carryg_s[f]
        extv_s[0:CONV_PAD, :] = carryv_s[f]

    def conv(ext_s, cw_ref, cb_ref, cs):
        out = cb_ref[:, cs]
        for t in range(CONV_W):
            lag = CONV_W - 1 - t
            out = out + cw_ref[t:t + 1, cs] * ext_s[CONV_PAD - lag:CONV_PAD - lag + tm, cs]
        return out

    subs = [slice(s * FF_SUB, (s + 1) * FF_SUB) for s in range(FF_TILE // FF_SUB)]
    for cs in subs:
        extg_s[CONV_PAD:, cs] = jnp.dot(h_s[...], wug_ref[:, cs], preferred_element_type=F32)
    for cs in subs:
        extv_s[CONV_PAD:, cs] = jnp.dot(h_s[...], wuv_ref[:, cs], preferred_element_type=F32)
    gates = [jax.nn.silu(conv(extg_s, cwg_ref, cbg_ref, cs)) for cs in subs]
    down = None
    for gate, cs in zip(gates, subs):
        act = (gate * conv(extv_s, cwv_ref, cbv_ref, cs)).astype(BF16)
        part = jnp.dot(act, wd_ref[cs, :], preferred_element_type=F32)
        down = part if down is None else down + part
    o_ref[...] += down

    carryg_s[f] = extg_s[tm:tm + CONV_PAD, :]
    carryv_s[f] = extv_s[tm:tm + CONV_PAD, :]
    cg_ref[f] = extg_s[tm + CONV_PAD - (CONV_W - 1):tm + CONV_PAD, :]
    cv_ref[f] = extv_s[tm + CONV_PAD - (CONV_W - 1):tm + CONV_PAD, :]

    if final_norm:
        @pl.when(f == pl.num_programs(1) - 1)
        def _():
            o_ref[...] = _rms(o_ref[...], fg_ref[...])


def _ffn_prompt(l, x, gn, wug, wuv, wd, cw, cb, fg, batch, seq, tm, final_norm):
    m = batch * seq
    tiles_per_seq = seq // tm
    row = lambda i, f: (i, 0)
    state = lambda i, f: (i // tiles_per_seq, 0, 0, 0)
    return pl.pallas_call(
        functools.partial(_ffn_prompt_kernel, final_norm, tiles_per_seq),
        grid=(m // tm, N_FF),
        in_specs=[pl.BlockSpec((tm, D_MODEL), row),
                  _of_layer(l, (1, D_MODEL)), _resident((1, D_MODEL))]
                 + _ffn_cast_weight_specs(lambda i, f: f) + _ffn_conv_specs(l, lambda i, f: f),
        out_specs=[pl.BlockSpec((tm, D_MODEL), row),
                   pl.BlockSpec((None, N_FF, CONV_W - 1, FF_TILE), state),
                   pl.BlockSpec((None, N_FF, CONV_W - 1, FF_TILE), state)],
        out_shape=[jax.ShapeDtypeStruct((m, D_MODEL), F32),
                   jax.ShapeDtypeStruct((batch, N_FF, CONV_W - 1, FF_TILE), F32),
                   jax.ShapeDtypeStruct((batch, N_FF, CONV_W - 1, FF_TILE), F32)],
        scratch_shapes=[pltpu.VMEM((tm, D_MODEL), BF16),
                        pltpu.VMEM((tm + CONV_PAD, FF_TILE), F32),
                        pltpu.VMEM((tm + CONV_PAD, FF_TILE), F32),
                        pltpu.VMEM((N_FF, CONV_PAD, FF_TILE), F32),
                        pltpu.VMEM((N_FF, CONV_PAD, FF_TILE), F32)],
        compiler_params=_params(("arbitrary", "arbitrary"), 62),
        name="ffn_prompt",
    )(x, gn, fg, wug, wuv, wd, cw, cw, cb, cb)


CAST_ROWS = 256


def _cast_weight(w_ref, wb_ref):
    for r in range(0, w_ref.shape[0], CAST_ROWS):
        wb_ref[r:r + CAST_ROWS, :] = w_ref[r:r + CAST_ROWS, :].astype(BF16)


def _ffn_sample_kernel(final_norm,
                       x_ref, gn_ref, fg_ref, wug_ref, wuv_ref, wd_ref, cwg_ref, cwv_ref, cbg_ref, cbv_ref,
                       s0g_ref, s0v_ref, s1g_ref, s1v_ref,
                       o_ref, ug_ref, uv_ref, wugb_ref, wuvb_ref, wdb_ref, h_s):
    f = pl.program_id(0)

    @pl.when(f == 0)
    def _():
        x = x_ref[...]
        h_s[...] = _rms(x, gn_ref[...]).astype(BF16)
        o_ref[...] = x

    def conv(w_ref, wb_ref, cw_ref, cb_ref, s0_ref, s1_ref, up_ref):
        _cast_weight(w_ref, wb_ref)
        up = jnp.dot(h_s[...], wb_ref[...], preferred_element_type=F32)
        up_ref[...] = up
        return ((cb_ref[...] + cw_ref[0:1, :] * s0_ref[...]) + cw_ref[1:2, :] * s1_ref[...]) + cw_ref[2:3, :] * up

    gate = conv(wug_ref, wugb_ref, cwg_ref, cbg_ref, s0g_ref, s1g_ref, ug_ref)
    val = conv(wuv_ref, wuvb_ref, cwv_ref, cbv_ref, s0v_ref, s1v_ref, uv_ref)
    act = (jax.nn.silu(gate) * val).astype(BF16)
    _cast_weight(wd_ref, wdb_ref)
    o_ref[...] += jnp.dot(act, wdb_ref[...], preferred_element_type=F32)

    if final_norm:
        @pl.when(f == pl.num_programs(0) - 1)
        def _():
            o_ref[...] = _rms(o_ref[...], fg_ref[...])


def _ffn_sample(l, x, gn, wu, wd, cw, cb, fg, s0, s1, final_norm):
    n = x.shape[0]
    gcol = lambda f: (l, 0, f)
    vcol = lambda f: (l, 0, N_FF + f)
    st = lambda im: pl.BlockSpec((None, n, FF_TILE), im)
    return pl.pallas_call(
        functools.partial(_ffn_sample_kernel, final_norm),
        grid=(N_FF,),
        in_specs=[_resident((n, D_MODEL)), _of_layer(l, (1, D_MODEL)), _resident((1, D_MODEL)),
                  pl.BlockSpec((None, D_MODEL, FF_TILE), gcol), pl.BlockSpec((None, D_MODEL, FF_TILE), vcol),
                  pl.BlockSpec((None, FF_TILE, D_MODEL), lambda f: (l, f, 0))]
                 + _ffn_conv_specs(l, lambda f: f)
                 + [st(gcol), st(vcol), st(gcol), st(vcol)],
        out_specs=[pl.BlockSpec((n, D_MODEL), lambda f: (0, 0)),
                   pl.BlockSpec((n, FF_TILE), lambda f: (0, f)),
                   pl.BlockSpec((n, FF_TILE), lambda f: (0, f))]
                  + _ffn_cast_weight_specs(lambda f: f),
        out_shape=[jax.ShapeDtypeStruct((n, D_MODEL), F32),
                   jax.ShapeDtypeStruct((n, D_FF), F32),
                   jax.ShapeDtypeStruct((n, D_FF), F32),
                   jax.ShapeDtypeStruct((D_MODEL, D_FF), BF16),
                   jax.ShapeDtypeStruct((D_MODEL, D_FF), BF16),
                   jax.ShapeDtypeStruct((D_FF, D_MODEL), BF16)],
        scratch_shapes=[pltpu.VMEM((n, D_MODEL), BF16)],
        compiler_params=_params(("arbitrary",), 44),
        name="ffn_sample",
    )(x, gn, fg, wu, wu, wd, cw, cw, cb, cb, s0, s0, s1, s1)


def _rope_tables(pos):
    half = HEAD_DIM // 2
    inv = ROPE_THETA ** (-jnp.arange(half, dtype=F32) / half)
    ang = pos.astype(F32)[:, None] * inv[None, :]
    cos, sin = jnp.cos(ang), jnp.sin(ang)
    reps = LANES // HEAD_DIM
    return (jnp.concatenate([cos, cos] * reps, axis=1),
            jnp.concatenate([-sin, sin] * reps, axis=1))


def _ssm_params(a_re, a_im, b_re, b_im, c_re, c_im, d, log_dt):
    depth = a_re.shape[0]
    dt = jnp.exp(log_dt)[..., None]
    mag = jnp.exp(a_re * dt)
    lr, li = mag * jnp.cos(a_im * dt), mag * jnp.sin(a_im * dt)
    nr, ni = lr - 1.0, li
    den = a_re * a_re + a_im * a_im
    qr = (nr * a_re + ni * a_im) / den
    qi = (ni * a_re - nr * a_im) / den
    bbr = qr[..., None] * b_re - qi[..., None] * b_im
    bbi = qr[..., None] * b_im + qi[..., None] * b_re

    def compact(t):
        r, c = t.shape[2], t.shape[3]
        t = t.reshape(depth, N_CHUNK, GROUP_CHUNK * r, c).astype(BF16)
        return jnp.pad(t, ((0, 0), (0, 0), (0, 0), (0, LANES - c)))

    def lane_tiling(c):
        src = jnp.arange(LANES)[:, None]
        dst = jnp.arange(GROUP_CHUNK * c)[None, :] % c
        return (src == dst).astype(BF16)

    return dict(bre=compact(bbr.transpose(0, 1, 3, 2)), bim=compact(bbi.transpose(0, 1, 3, 2)),
                cre=compact(c_re.transpose(0, 1, 3, 2)), cimn=compact((-c_im).transpose(0, 1, 3, 2)),
                tile_b=lane_tiling(SSM_STATE), tile_c=lane_tiling(SSM_GROUP),
                lre=lr.reshape(depth, 1, N_STATE), lim=li.reshape(depth, 1, N_STATE),
                d=d.reshape(depth, 1, SSM_WIDTH))


def _segment_permutation():
    r = jnp.arange(SSM_TILE)
    src = (r % SUBLANES) * SEG_LEN + r // SUBLANES
    return (src[:, None] == jnp.arange(SSM_TILE)[None, :]).astype(BF16)


def kernel(x_prompt, x_sample, cache_k, cache_v, state_ssm_re, state_ssm_im, state_conv, attn_norm_g, w_in, attn_sinks, ssm_a_re, ssm_a_im, ssm_b_re, ssm_b_im, ssm_c_re, ssm_c_im, ssm_d, ssm_log_dt, w_glu, b_glu, attn_out_norm_g, ssm_out_norm_g, w_out, ffn_norm_g, w_up, conv_w, conv_b, w_down, final_norm_g):
    batch, seq, _ = x_prompt.shape
    nsamp, dec_seq, _ = x_sample.shape
    wbuf = cache_k.shape[2]
    assert dec_seq == 1 and wbuf == WINDOW and seq % (SSM_SUBTILES * SSM_TILE) == 0
    assert PAST_LEN >= wbuf

    w_in_b, w_glu_b, w_out_b = w_in.astype(BF16), w_glu.astype(BF16), w_out.astype(BF16)
    ssm = _ssm_params(ssm_a_re, ssm_a_im, ssm_b_re, ssm_b_im, ssm_c_re, ssm_c_im, ssm_d, ssm_log_dt)
    perm = _segment_permutation()

    cos_p, sin_p = _rope_tables(jnp.arange(seq, dtype=jnp.int32))
    cos_s, sin_s = _rope_tables(jnp.full((nsamp,), PAST_LEN, dtype=jnp.int32))

    rows = lambda a: a.reshape(a.shape[0], 1, a.shape[-1])
    g_in, g_a, g_s, g_f = rows(attn_norm_g), rows(attn_out_norm_g), rows(ssm_out_norm_g), rows(ffn_norm_g)
    bg, cb = rows(b_glu), rows(conv_b)
    fg = final_norm_g.reshape(1, D_MODEL)
    sink_col = attn_sinks.reshape(DEPTH, N_HEADS, 1)
    head_is_lo = (jnp.arange(N_HEADS) < KV_REP)[None, :, None]
    kc = cache_k.reshape(DEPTH, nsamp, wbuf, KV_COLS)
    vc = cache_v.reshape(DEPTH, nsamp, wbuf, KV_COLS)
    h0re = state_ssm_re.reshape(DEPTH, nsamp, N_STATE)
    h0im = state_ssm_im.reshape(DEPTH, nsamp, N_STATE)
    conv0, conv1 = state_conv[:, :, 0, :], state_conv[:, :, 1, :]

    xp = x_prompt.reshape(batch * seq, D_MODEL)
    xs = x_sample.reshape(nsamp, D_MODEL)
    tm_p = 512
    tm_ffn = 1024
    outs = {k: [] for k in ("kp", "vp", "hrp", "hip", "cgp", "cvp", "ks", "vs", "hrs", "his", "ugs", "uvs")}

    for l in range(DEPTH):
        last = l == DEPTH - 1

        q, k, v, u = _in_proj(l, xs, g_in, w_in_b, cos_s, sin_s, nsamp)
        qh = q.reshape(nsamp, N_HEADS, HEAD_DIM)
        zq = jnp.zeros_like(qh)
        qbd = jnp.where(head_is_lo, jnp.concatenate([qh, zq], axis=-1), jnp.concatenate([zq, qh], axis=-1))
        o = _attn_sample(l, qbd, kc, vc, k.reshape(nsamp, 1, KV_COLS), v.reshape(nsamp, 1, KV_COLS), sink_col, 8)
        a_raw = jnp.concatenate([o[:, :KV_REP, :HEAD_DIM].reshape(nsamp, -1),
                                 o[:, KV_REP:, HEAD_DIM:].reshape(nsamp, -1)], axis=1)
        a = _rms_cast(l, a_raw, g_a)
        s, hre, him = _ssm_sample(l, u, h0re, h0im, ssm, w_glu_b, bg, g_s)
        x1 = _out_proj(l, xs, a, s, w_out_b, nsamp)
        xs, ug, uv, wug_b, wuv_b, wd_b = _ffn_sample(l, x1, g_f, w_up, w_down, conv_w, cb, fg, conv0, conv1, last)
        outs["ks"].append(k)
        outs["vs"].append(v)
        outs["hrs"].append(hre)
        outs["his"].append(him)
        outs["ugs"].append(ug)
        outs["uvs"].append(uv)

        q, k, v, u = _in_proj(l, xp, g_in, w_in_b, cos_p, sin_p, tm_p)
        a = _attn_prompt(l, q, k, v, attn_sinks, g_a, batch, seq)
        s, hre, him = _ssm_prompt(l, u, perm, ssm, w_glu_b, bg, g_s, batch, seq)
        x1 = _out_proj(l, xp, a, s, w_out_b, tm_p)
        xp, cg, cv = _ffn_prompt(l, x1, g_f, wug_b, wuv_b, wd_b, conv_w, cb, fg, batch, seq, tm_ffn, last)
        outs["kp"].append(k.reshape(batch, seq, KV_COLS)[:, seq - WINDOW:])
        outs["vp"].append(v.reshape(batch, seq, KV_COLS)[:, seq - WINDOW:])
        outs["hrp"].append(hre)
        outs["hip"].append(him)
        outs["cgp"].append(cg)
        outs["cvp"].append(cv)

    st = lambda name: jnp.stack(outs[name], axis=0)
    heads = lambda t: t.reshape(t.shape[:-1] + (N_KV_HEADS, HEAD_DIM))
    states = lambda t: t.reshape(DEPTH, -1, SSM_GROUPS, SSM_STATE)
    unblock = lambda t: t.transpose(0, 1, 3, 2, 4).reshape(DEPTH, batch, CONV_W - 1, D_FF)
    conv_prompt = jnp.concatenate([unblock(st("cgp")), unblock(st("cvp"))], axis=-1)
    k_sample = jnp.concatenate([cache_k[:, :, 1:], heads(st("ks"))[:, :, None]], axis=2)
    v_sample = jnp.concatenate([cache_v[:, :, 1:], heads(st("vs"))[:, :, None]], axis=2)
    conv_sample = jnp.stack([conv1, jnp.concatenate([st("ugs"), st("uvs")], axis=-1)], axis=2)
    return (xp.reshape(batch, seq, D_MODEL), xs.reshape(nsamp, 1, D_MODEL),
            heads(st("kp")), heads(st("vp")), states(st("hrp")), states(st("hip")), conv_prompt,
            k_sample, v_sample, states(st("hrs")), states(st("his")), conv_sample)
```

```python
import functools

import jax
import jax.numpy as jnp
from jax import lax
from jax.experimental import pallas as pl
from jax.experimental.pallas import tpu as pltpu

F32 = jnp.float32
BF16 = jnp.bfloat16

D_MODEL = 2048
DEPTH = 4
ATTN_WIDTH = 1024
SSM_WIDTH = 1024
HEAD_DIM = 64
N_HEADS = 16
N_KV_HEADS = 2
KV_REP = 8
KV_COLS = N_KV_HEADS * HEAD_DIM
WINDOW = 128
ROPE_THETA = 10000.0
SSM_GROUP = 16
SSM_GROUPS = 64
SSM_STATE = 64
N_STATE = SSM_GROUPS * SSM_STATE
D_FF = 5632
CONV_W = 3
RMS_EPS = 1e-6
IN_COLS = ATTN_WIDTH + 2 * KV_COLS + SSM_WIDTH
NEG_INF = -1e30
PAST_LEN = 16384
SCORE_SCALE = HEAD_DIM ** -0.5
assert SCORE_SCALE == 2.0 ** -3

LANES = 128
SUBLANES = 8
MXU_DIM = 256
MIB = 1024 * 1024
U_BLOCKS = SSM_WIDTH // LANES

GROUP_CHUNK = 16
N_CHUNK = SSM_GROUPS // GROUP_CHUNK
CH_U = GROUP_CHUNK * SSM_GROUP
CH_S = GROUP_CHUNK * SSM_STATE
BLOCKS_PER_CHUNK = CH_U // LANES

SEG_LEN = 32
SSM_TILE = SUBLANES * SEG_LEN
SSM_SUBTILES = 2
SCAN_LANES = 512


def _rms(x, g):
    ms = jnp.mean(x * x, axis=-1, keepdims=True)
    return x * lax.rsqrt(ms + RMS_EPS) * g


def _params(sem, vmem_mib):
    return pltpu.CompilerParams(dimension_semantics=sem, vmem_limit_bytes=vmem_mib * MIB)


def _resident(shape):
    nd = len(shape)
    return pl.BlockSpec(shape, lambda *_: (0,) * nd, pipeline_mode=pl.Buffered(1))


def _of_layer(l, shape):
    nd = len(shape)
    return pl.BlockSpec((None,) + tuple(shape), lambda *_: (l,) + (0,) * nd, pipeline_mode=pl.Buffered(1))


def _in_proj_kernel(x_ref, g_ref, w_ref, cos_ref, sin_ref, q_ref, k_ref, v_ref, u_ref):
    h = _rms(x_ref[...], g_ref[...]).astype(BF16)
    proj = jnp.dot(h, w_ref[...], preferred_element_type=F32)
    cos = cos_ref[...]
    sin = sin_ref[...]
    lane = lax.broadcasted_iota(jnp.int32, cos.shape, 1)
    first_half = (lane & (HEAD_DIM - 1)) < (HEAD_DIM // 2)

    def rope(blk):
        partner = jnp.where(first_half,
                            pltpu.roll(blk, LANES - HEAD_DIM // 2, 1),
                            pltpu.roll(blk, HEAD_DIM // 2, 1))
        return blk * cos + partner * sin

    for j in range(ATTN_WIDTH // LANES):
        q_ref[:, j * LANES:(j + 1) * LANES] = (rope(proj[:, j * LANES:(j + 1) * LANES]) * SCORE_SCALE).astype(BF16)
    k_ref[...] = rope(proj[:, ATTN_WIDTH:ATTN_WIDTH + KV_COLS])
    v_ref[...] = proj[:, ATTN_WIDTH + KV_COLS:ATTN_WIDTH + 2 * KV_COLS]
    u0 = ATTN_WIDTH + 2 * KV_COLS
    for j in range(U_BLOCKS):
        u_ref[j] = proj[:, u0 + j * LANES:u0 + (j + 1) * LANES]


def _in_proj(l, x, g, w, cos, sin, tm):
    m = x.shape[0]
    pos_tiles = cos.shape[0] // tm
    row = lambda i: (i, 0)
    return pl.pallas_call(
        _in_proj_kernel,
        grid=(m // tm,),
        in_specs=[pl.BlockSpec((tm, D_MODEL), row),
                  _of_layer(l, (1, D_MODEL)),
                  _of_layer(l, (D_MODEL, IN_COLS)),
                  pl.BlockSpec((tm, LANES), lambda i: (i % pos_tiles, 0)),
                  pl.BlockSpec((tm, LANES), lambda i: (i % pos_tiles, 0))],
        out_specs=[pl.BlockSpec((tm, ATTN_WIDTH), row),
                   pl.BlockSpec((tm, KV_COLS), row),
                   pl.BlockSpec((tm, KV_COLS), row),
                   pl.BlockSpec((U_BLOCKS, tm, LANES), lambda i: (0, i, 0))],
        out_shape=[jax.ShapeDtypeStruct((m, ATTN_WIDTH), BF16),
                   jax.ShapeDtypeStruct((m, KV_COLS), F32),
                   jax.ShapeDtypeStruct((m, KV_COLS), F32),
                   jax.ShapeDtypeStruct((U_BLOCKS, m, LANES), F32)],
        compiler_params=_params(("arbitrary",), 60),
        name="in_proj",
    )(x, g, w, cos, sin)


ATTN_SUB = 4


def _attn_prompt_kernel(layer, sinks_ref, q_ref, kc_ref, kp_ref, vc_ref, vp_ref, gn_ref, a_ref):
    j = pl.program_id(1)
    lane = lax.broadcasted_iota(jnp.int32, (WINDOW, LANES), 1)
    lo = lane < HEAD_DIM
    qi = lax.broadcasted_iota(jnp.int32, (WINDOW, 2 * WINDOW), 0)
    kj = lax.broadcasted_iota(jnp.int32, (WINDOW, 2 * WINDOW), 1)
    diff = qi + WINDOW - kj
    band = (diff >= 0) & (diff <= WINDOW)
    first_key = jnp.where(j > 0, 0, WINDOW)
    valid = [band & (kj >= first_key)] + [band] * (ATTN_SUB - 1)
    zero = jnp.zeros((WINDOW, LANES), F32)

    def halves(x, g):
        r = pltpu.roll(x, HEAD_DIM, 1)
        if g == 0:
            return jnp.where(lo, x, zero).astype(BF16), jnp.where(lo, zero, r).astype(BF16)
        return jnp.where(lo, r, zero).astype(BF16), jnp.where(lo, zero, x).astype(BF16)

    rows = lambda sb: slice(sb * WINDOW, (sb + 1) * WINDOW)
    k_blocks = [kp_ref[...]] + [kc_ref[rows(sb), :] for sb in range(ATTN_SUB)]
    v_blocks = [vp_ref[...]] + [vc_ref[rows(sb), :] for sb in range(ATTN_SUB)]
    k_halves = [[halves(x, g) for x in k_blocks] for g in range(N_KV_HEADS)]
    v_halves = [[halves(x, g) for x in v_blocks] for g in range(N_KV_HEADS)]
    pairs = KV_REP // 2
    nt = (((1,), (1,)), ((), ()))

    for sb in range(ATTN_SUB):
        def softmax(s, head, sb=sb):
            s = jnp.where(valid[sb], s, NEG_INF)
            sink = sinks_ref[layer, head]
            m = jnp.maximum(jnp.max(s, axis=-1, keepdims=True), sink)
            p = jnp.exp(s - m)
            denom = jnp.sum(p, axis=-1, keepdims=True) + jnp.exp(sink - m)
            return (p / denom).astype(BF16)

        out_blocks = []
        for g in range(N_KV_HEADS):
            window = lambda hv, half: jnp.concatenate([hv[g][sb][half], hv[g][sb + 1][half]], axis=0)
            k_lo, k_hi, v_lo, v_hi = window(k_halves, 0), window(k_halves, 1), window(v_halves, 0), window(v_halves, 1)
            qg = jnp.concatenate([q_ref[rows(sb), (g * pairs + p) * LANES:(g * pairs + p + 1) * LANES]
                                  for p in range(pairs)], axis=0)
            s_even = lax.dot_general(qg, k_lo, nt, preferred_element_type=F32)
            s_odd = lax.dot_general(qg, k_hi, nt, preferred_element_type=F32)
            p_even = jnp.concatenate(
                [softmax(s_even[rows(p)], g * KV_REP + 2 * p) for p in range(pairs)], axis=0)
            p_odd = jnp.concatenate(
                [softmax(s_odd[rows(p)], g * KV_REP + 2 * p + 1) for p in range(pairs)], axis=0)
            o = (jnp.dot(p_even, v_lo, preferred_element_type=F32)
                 + jnp.dot(p_odd, v_hi, preferred_element_type=F32))
            out_blocks += [o[rows(p)] for p in range(pairs)]

        ssq = out_blocks[0] * out_blocks[0]
        for blk in out_blocks[1:]:
            ssq = ssq + blk * blk
        inv = lax.rsqrt(jnp.sum(ssq, axis=-1, keepdims=True) / ATTN_WIDTH + RMS_EPS)
        for i, blk in enumerate(out_blocks):
            cols = slice(i * LANES, (i + 1) * LANES)
            a_ref[rows(sb), cols] = (blk * inv * gn_ref[:, cols]).astype(BF16)


def _attn_prompt(l, q, k, v, sinks, gn, batch, seq):
    nb = seq // WINDOW
    steps = nb // ATTN_SUB
    cur = lambda b, j: (b * steps + j, 0)
    prev = lambda b, j: (b * nb + jnp.maximum(j * ATTN_SUB - 1, 0), 0)
    return pl.pallas_call(
        functools.partial(_attn_prompt_kernel, l),
        grid=(batch, steps),
        in_specs=[pl.BlockSpec(memory_space=pltpu.SMEM),
                  pl.BlockSpec((ATTN_SUB * WINDOW, ATTN_WIDTH), cur),
                  pl.BlockSpec((ATTN_SUB * WINDOW, KV_COLS), cur), pl.BlockSpec((WINDOW, KV_COLS), prev),
                  pl.BlockSpec((ATTN_SUB * WINDOW, KV_COLS), cur), pl.BlockSpec((WINDOW, KV_COLS), prev),
                  _of_layer(l, (1, ATTN_WIDTH))],
        out_specs=pl.BlockSpec((ATTN_SUB * WINDOW, ATTN_WIDTH), cur),
        out_shape=jax.ShapeDtypeStruct((batch * seq, ATTN_WIDTH), BF16),
        compiler_params=_params(("arbitrary", "arbitrary"), 32),
        name="attn_prompt",
    )(sinks, q, k, k, v, v, gn)


def _attn_sample_kernel(qbd_ref, kc_ref, vc_ref, kn_ref, vn_ref, sink_ref, o_ref):
    bb = qbd_ref.shape[0]
    rows, keys = bb * N_HEADS, bb * WINDOW
    q = qbd_ref[...].reshape(rows, LANES)
    kc = kc_ref[...].reshape(keys, KV_COLS).astype(BF16)
    vc = vc_ref[...].reshape(keys, KV_COLS).astype(BF16)
    per_head = lambda ref: jnp.broadcast_to(ref[...], (bb, N_HEADS, KV_COLS)).reshape(rows, KV_COLS)
    kn, vn = per_head(kn_ref), per_head(vn_ref)
    sink = jnp.concatenate([sink_ref[...]] * bb, axis=0)
    nt = (((1,), (1,)), ((), ()))
    s = lax.dot_general(q, kc, nt, preferred_element_type=F32)
    own = ((lax.broadcasted_iota(jnp.int32, s.shape, 0) >> _log2(N_HEADS))
           == (lax.broadcasted_iota(jnp.int32, s.shape, 1) >> _log2(WINDOW)))
    s = jnp.where(own, s, NEG_INF)
    s_new = jnp.sum(q.astype(F32) * kn, axis=-1, keepdims=True)
    m = jnp.maximum(jnp.maximum(jnp.max(s, axis=-1, keepdims=True), s_new), sink)
    p = jnp.exp(s - m)
    p_new = jnp.exp(s_new - m)
    denom = jnp.sum(p, axis=-1, keepdims=True) + p_new + jnp.exp(sink - m)
    o = jnp.dot((p / denom).astype(BF16), vc, preferred_element_type=F32) + (p_new / denom) * vn
    o_ref[...] = o.reshape(bb, N_HEADS, LANES)


def _attn_sample(l, qbd, kc, vc, kn, vn, sink_col, bb):
    nbatch = qbd.shape[0]
    blk = lambda *s: pl.BlockSpec((bb,) + s, lambda i: (i, 0, 0))
    cache = pl.BlockSpec((None, bb, WINDOW, KV_COLS), lambda i: (l, i, 0, 0))
    return pl.pallas_call(
        _attn_sample_kernel,
        grid=(nbatch // bb,),
        in_specs=[blk(N_HEADS, LANES), cache, cache,
                  blk(1, KV_COLS), blk(1, KV_COLS), _of_layer(l, (N_HEADS, 1))],
        out_specs=blk(N_HEADS, LANES),
        out_shape=jax.ShapeDtypeStruct((nbatch, N_HEADS, LANES), F32),
        compiler_params=_params(("arbitrary",), 32),
        name="attn_sample",
    )(qbd, kc, vc, kn, vn, sink_col)


def _rms_cast_kernel(x_ref, g_ref, o_ref):
    o_ref[...] = _rms(x_ref[...], g_ref[...]).astype(BF16)


def _rms_cast(l, x, g):
    n, d = x.shape
    return pl.pallas_call(
        _rms_cast_kernel,
        grid=(1,),
        in_specs=[_resident((n, d)), _of_layer(l, (1, d))],
        out_specs=pl.BlockSpec((n, d), lambda i: (0, 0)),
        out_shape=jax.ShapeDtypeStruct(x.shape, BF16),
        name="rms_cast",
    )(x, g)


def _glu_norm(z, wg_ref, bg_ref, gn_ref):
    gate = jax.nn.sigmoid(jnp.dot(z.astype(BF16), wg_ref[...], preferred_element_type=F32) + bg_ref[...])
    return _rms(z * gate, gn_ref[...])


def _ssm_operands(l, ssm, wg, bg, gn):
    layered = (ssm["bre"], ssm["bim"], ssm["cre"], ssm["cimn"], ssm["lre"], ssm["lim"], ssm["d"], wg, bg, gn)
    shared = (ssm["tile_b"], ssm["tile_c"])
    return (layered + shared,
            [_of_layer(l, a.shape[1:]) for a in layered] + [_resident(a.shape) for a in shared])


def _log2(n):
    assert n & (n - 1) == 0
    return n.bit_length() - 1


def _block_diag_scratch():
    return [pltpu.VMEM((N_CHUNK, CH_U, CH_S), BF16), pltpu.VMEM((N_CHUNK, CH_U, CH_S), BF16),
            pltpu.VMEM((N_CHUNK, CH_S, CH_U), BF16), pltpu.VMEM((N_CHUNK, CH_S, CH_U), BF16)]


def _expand_block_diag(t_ref, tile_ref, out_s, rows_per_group, cols_per_group):
    for c in range(N_CHUNK):
        tiled = jnp.dot(t_ref[c], tile_ref[...], preferred_element_type=F32)
        row_group = lax.broadcasted_iota(jnp.int32, tiled.shape, 0) >> _log2(rows_per_group)
        col_group = lax.broadcasted_iota(jnp.int32, tiled.shape, 1) >> _log2(cols_per_group)
        out_s[c] = jnp.where(row_group == col_group, tiled, 0.0).astype(BF16)


def _expand_projections(tb_re, tb_im, tc_re, tc_imn, tile_b, tile_c, bre_s, bim_s, cre_s, cimn_s):
    _expand_block_diag(tb_re, tile_b, bre_s, SSM_GROUP, SSM_STATE)
    _expand_block_diag(tb_im, tile_b, bim_s, SSM_GROUP, SSM_STATE)
    _expand_block_diag(tc_re, tile_c, cre_s, SSM_STATE, SSM_GROUP)
    _expand_block_diag(tc_imn, tile_c, cimn_s, SSM_STATE, SSM_GROUP)


def _ssm_prompt_kernel(u_ref, perm_ref, tb_re, tb_im, tc_re, tc_imn, lre_ref, lim_ref, d_ref,
                       wg_ref, bg_ref, gn_ref, tile_b, tile_c,
                       s_ref, hre_ref, him_ref,
                       bre_ref, bim_ref, cre_ref, cimn_ref,
                       xre_s, xim_s, hbre_s, hbim_s, pre_s, pim_s, cre_s, cim_s, hinre_s, hinim_s, y_s):
    b = pl.program_id(0)
    j = pl.program_id(1)

    @pl.when((b == 0) & (j == 0))
    def _():
        _expand_projections(tb_re, tb_im, tc_re, tc_imn, tile_b, tile_c, bre_ref, bim_ref, cre_ref, cimn_ref)
        lr, li = lre_ref[...], lim_ref[...]
        pr, pi = lr, li
        for i in range(SEG_LEN):
            if i:
                pr, pi = pr * lr - pi * li, pr * li + pi * lr
            pre_s[i * SUBLANES:(i + 1) * SUBLANES, :] = jnp.broadcast_to(pr, (SUBLANES, N_STATE))
            pim_s[i * SUBLANES:(i + 1) * SUBLANES, :] = jnp.broadcast_to(pi, (SUBLANES, N_STATE))

    @pl.when(j == 0)
    def _():
        cre_s[...] = jnp.zeros_like(cre_s)
        cim_s[...] = jnp.zeros_like(cim_s)

    u_nat = jnp.concatenate([u_ref[jb] for jb in range(U_BLOCKS)], axis=1)
    u_b = u_nat.astype(BF16)
    u_perm = jnp.concatenate(
        [jnp.dot(perm_ref[...], u_b[t * SSM_TILE:(t + 1) * SSM_TILE], preferred_element_type=F32).astype(BF16)
         for t in range(SSM_SUBTILES)], axis=0)

    last = (SEG_LEN - 1) * SUBLANES
    for c in range(N_CHUNK):
        uc = u_perm[:, c * CH_U:(c + 1) * CH_U]
        chunk = slice(c * CH_S, (c + 1) * CH_S)
        xre_s[:, chunk] = jnp.dot(uc, bre_ref[c], preferred_element_type=F32)
        xim_s[:, chunk] = jnp.dot(uc, bim_ref[c], preferred_element_type=F32)

        for t in range(SSM_SUBTILES):
            for lc in range(CH_S // SCAN_LANES):
                sl = slice(c * CH_S + lc * SCAN_LANES, c * CH_S + (lc + 1) * SCAN_LANES)
                lr = jnp.broadcast_to(lre_ref[:, sl], (SUBLANES, SCAN_LANES))
                li = jnp.broadcast_to(lim_ref[:, sl], (SUBLANES, SCAN_LANES))
                r0 = t * SSM_TILE
                hr = xre_s[r0:r0 + SUBLANES, sl]
                hi = xim_s[r0:r0 + SUBLANES, sl]
                for i in range(1, SEG_LEN):
                    rows = slice(r0 + i * SUBLANES, r0 + (i + 1) * SUBLANES)
                    hr, hi = (lr * hr - li * hi) + xre_s[rows, sl], (lr * hi + li * hr) + xim_s[rows, sl]
                    xre_s[rows, sl] = hr
                    xim_s[rows, sl] = hi

        l32r = pre_s[last:last + 1, chunk]
        l32i = pim_s[last:last + 1, chunk]
        cr, ci = cre_s[:, chunk], cim_s[:, chunk]
        for seg in range(SSM_SUBTILES * SUBLANES):
            hinre_s[seg:seg + 1, chunk] = cr
            hinim_s[seg:seg + 1, chunk] = ci
            end_row = (seg // SUBLANES) * SSM_TILE + last + seg % SUBLANES
            er = xre_s[end_row:end_row + 1, chunk]
            ei = xim_s[end_row:end_row + 1, chunk]
            cr, ci = (l32r * cr - l32i * ci) + er, (l32r * ci + l32i * cr) + ei
        cre_s[:, chunk] = cr
        cim_s[:, chunk] = ci

        for t in range(SSM_SUBTILES):
            for lc in range(CH_S // SCAN_LANES):
                sl = slice(c * CH_S + lc * SCAN_LANES, c * CH_S + (lc + 1) * SCAN_LANES)
                hr_in = jnp.concatenate([hinre_s[t * SUBLANES:(t + 1) * SUBLANES, sl]] * 2, axis=0)
                hi_in = jnp.concatenate([hinim_s[t * SUBLANES:(t + 1) * SUBLANES, sl]] * 2, axis=0)
                for i2 in range(SEG_LEN // 2):
                    prow = slice(i2 * 2 * SUBLANES, (i2 + 1) * 2 * SUBLANES)
                    rows = slice(t * SSM_TILE + prow.start, t * SSM_TILE + prow.stop)
                    pr, pi = pre_s[prow, sl], pim_s[prow, sl]
                    hbre_s[rows, sl] = (xre_s[rows, sl] + (pr * hr_in - pi * hi_in)).astype(BF16)
                    hbim_s[rows, sl] = (xim_s[rows, sl] + (pr * hi_in + pi * hr_in)).astype(BF16)

        y = (jnp.dot(hbre_s[:, chunk], cre_ref[c], preferred_element_type=F32)
             + jnp.dot(hbim_s[:, chunk], cimn_ref[c], preferred_element_type=F32))
        for h in range(BLOCKS_PER_CHUNK):
            y_s[c * BLOCKS_PER_CHUNK + h] = y[:, h * LANES:(h + 1) * LANES]

    y_nat = jnp.concatenate(
        [jnp.concatenate([y_s[jb, pl.ds(t * SSM_TILE + i0 * SUBLANES + seg, SUBLANES, stride=SUBLANES), :]
                          for t in range(SSM_SUBTILES)
                          for seg in range(SUBLANES) for i0 in range(0, SEG_LEN, SUBLANES)], axis=0)
         for jb in range(U_BLOCKS)], axis=1)
    z = jax.nn.gelu(y_nat + d_ref[...] * u_nat)
    s_ref[...] = _glu_norm(z, wg_ref, bg_ref, gn_ref).astype(BF16)

    @pl.when(j == pl.num_programs(1) - 1)
    def _():
        hre_ref[...] = cre_s[...]
        him_ref[...] = cim_s[...]


def _ssm_prompt(l, u, perm, ssm, wg, bg, gn, batch, seq):
    step_rows = SSM_SUBTILES * SSM_TILE
    nt = seq // step_rows
    row = lambda b, j: (b * nt + j, 0)
    st = lambda b, j: (b, 0, 0)
    arrays, specs = _ssm_operands(l, ssm, wg, bg, gn)
    scratch = _block_diag_scratch() + [
        pltpu.VMEM((step_rows, N_STATE), F32),
        pltpu.VMEM((step_rows, N_STATE), F32),
        pltpu.VMEM((step_rows, N_STATE), BF16),
        pltpu.VMEM((step_rows, N_STATE), BF16),
        pltpu.VMEM((SSM_TILE, N_STATE), F32),
        pltpu.VMEM((SSM_TILE, N_STATE), F32),
        pltpu.VMEM((1, N_STATE), F32),
        pltpu.VMEM((1, N_STATE), F32),
        pltpu.VMEM((SSM_SUBTILES * SUBLANES, N_STATE), F32),
        pltpu.VMEM((SSM_SUBTILES * SUBLANES, N_STATE), F32),
        pltpu.VMEM((U_BLOCKS, step_rows, LANES), F32),
    ]
    return pl.pallas_call(
        _ssm_prompt_kernel,
        grid=(batch, nt),
        in_specs=[pl.BlockSpec((U_BLOCKS, step_rows, LANES), lambda b, j: (0, b * nt + j, 0)),
                  _resident((SSM_TILE, SSM_TILE))] + specs,
        out_specs=[pl.BlockSpec((step_rows, SSM_WIDTH), row),
                   pl.BlockSpec((None, 1, N_STATE), st),
                   pl.BlockSpec((None, 1, N_STATE), st)],
        out_shape=[jax.ShapeDtypeStruct((batch * seq, SSM_WIDTH), BF16),
                   jax.ShapeDtypeStruct((batch, 1, N_STATE), F32),
                   jax.ShapeDtypeStruct((batch, 1, N_STATE), F32)],
        scratch_shapes=scratch,
        compiler_params=_params(("arbitrary", "arbitrary"), 56),
        name="ssm_prompt",
    )(u, perm, *arrays)


def _ssm_sample_kernel(u_ref, h0re_ref, h0im_ref, tb_re, tb_im, tc_re, tc_imn, lre_ref, lim_ref, d_ref,
                       wg_ref, bg_ref, gn_ref, tile_b, tile_c, s_ref, hre_ref, him_ref,
                       bre_ref, bim_ref, cre_ref, cimn_ref):
    _expand_projections(tb_re, tb_im, tc_re, tc_imn, tile_b, tile_c, bre_ref, bim_ref, cre_ref, cimn_ref)
    u = jnp.concatenate([u_ref[jb] for jb in range(U_BLOCKS)], axis=1)
    zs = []
    for c in range(N_CHUNK):
        cs = slice(c * CH_U, (c + 1) * CH_U)
        ss = slice(c * CH_S, (c + 1) * CH_S)
        uc = u[:, cs].astype(BF16)
        lr, li = lre_ref[:, ss], lim_ref[:, ss]
        h0r, h0i = h0re_ref[:, ss], h0im_ref[:, ss]
        hr = jnp.dot(uc, bre_ref[c], preferred_element_type=F32) + (lr * h0r - li * h0i)
        hi = jnp.dot(uc, bim_ref[c], preferred_element_type=F32) + (lr * h0i + li * h0r)
        hre_ref[:, ss] = hr
        him_ref[:, ss] = hi
        y = (jnp.dot(hr.astype(BF16), cre_ref[c], preferred_element_type=F32)
             + jnp.dot(hi.astype(BF16), cimn_ref[c], preferred_element_type=F32))
        zs.append(jax.nn.gelu(y + d_ref[:, cs] * u[:, cs]))
    z = jnp.concatenate(zs, axis=1)
    s_ref[...] = _glu_norm(z, wg_ref, bg_ref, gn_ref).astype(BF16)


def _ssm_sample(l, u, h0re, h0im, ssm, wg, bg, gn):
    n = u.shape[1]
    arrays, specs = _ssm_operands(l, ssm, wg, bg, gn)
    whole = lambda *s: pl.BlockSpec(s, lambda i: (0,) * len(s))
    return pl.pallas_call(
        _ssm_sample_kernel,
        grid=(1,),
        in_specs=[_resident((U_BLOCKS, n, LANES)), _of_layer(l, (n, N_STATE)), _of_layer(l, (n, N_STATE))] + specs,
        out_specs=[whole(n, SSM_WIDTH), whole(n, N_STATE), whole(n, N_STATE)],
        out_shape=[jax.ShapeDtypeStruct((n, SSM_WIDTH), BF16),
                   jax.ShapeDtypeStruct((n, N_STATE), F32),
                   jax.ShapeDtypeStruct((n, N_STATE), F32)],
        scratch_shapes=_block_diag_scratch(),
        compiler_params=_params(("arbitrary",), 40),
        name="ssm_sample",
    )(u, h0re, h0im, *arrays)


def _out_proj_kernel(x_ref, a_ref, s_ref, w_ref, o_ref):
    acc = (jnp.dot(a_ref[...], w_ref[0:ATTN_WIDTH, :], preferred_element_type=F32)
           + jnp.dot(s_ref[...], w_ref[ATTN_WIDTH:, :], preferred_element_type=F32))
    o_ref[...] = x_ref[...] + acc


def _out_proj(l, x, a, s, w, tm):
    m = x.shape[0]
    row = lambda i: (i, 0)
    return pl.pallas_call(
        _out_proj_kernel,
        grid=(m // tm,),
        in_specs=[pl.BlockSpec((tm, D_MODEL), row),
                  pl.BlockSpec((tm, ATTN_WIDTH), row),
                  pl.BlockSpec((tm, SSM_WIDTH), row),
                  _of_layer(l, (D_MODEL, D_MODEL))],
        out_specs=pl.BlockSpec((tm, D_MODEL), row),
        out_shape=jax.ShapeDtypeStruct((m, D_MODEL), F32),
        compiler_params=_params(("arbitrary",), 60),
        name="out_proj",
    )(x, a, s, w)


FF_TILE = 512
N_FF = D_FF // FF_TILE
FF_SUB = MXU_DIM
CONV_PAD = SUBLANES


def _ffn_conv_specs(l, idx):
    gcol = lambda *g: (l, 0, idx(*g))
    vcol = lambda *g: (l, 0, N_FF + idx(*g))
    return [pl.BlockSpec((None, CONV_W, FF_TILE), gcol), pl.BlockSpec((None, CONV_W, FF_TILE), vcol),
            pl.BlockSpec((None, 1, FF_TILE), gcol), pl.BlockSpec((None, 1, FF_TILE), vcol)]


def _ffn_cast_weight_specs(idx):
    col = lambda *g: (0, idx(*g))
    return [pl.BlockSpec((D_MODEL, FF_TILE), col), pl.BlockSpec((D_MODEL, FF_TILE), col),
            pl.BlockSpec((FF_TILE, D_MODEL), lambda *g: (idx(*g), 0))]


def _ffn_prompt_kernel(final_norm, tiles_per_seq,
                       x_ref, gn_ref, fg_ref, wug_ref, wuv_ref, wd_ref, cwg_ref, cwv_ref, cbg_ref, cbv_ref,
                       o_ref, cg_ref, cv_ref,
                       h_s, extg_s, extv_s, carryg_s, carryv_s):
    m = pl.program_id(0)
    f = pl.program_id(1)
    tm = x_ref.shape[0]

    @pl.when(f == 0)
    def _():
        x = x_ref[...]
        h_s[...] = _rms(x, gn_ref[...]).astype(BF16)
        o_ref[...] = x

    seq_start = (m % tiles_per_seq) == 0

    @pl.when(seq_start)
    def _():
        extg_s[0:CONV_PAD, :] = jnp.zeros((CONV_PAD, FF_TILE), F32)
        extv_s[0:CONV_PAD, :] = jnp.zeros((CONV_PAD, FF_TILE), F32)

    @pl.when(jnp.logical_not(seq_start))
    def _():
        extg_s[0:CONV_PAD, :] = carryg_s[f]
        extv_s[0:CONV_PAD, :] = carryv_s[f]

    def conv(ext_s, cw_ref, cb_ref, cs):
        out = cb_ref[:, cs]
        for t in range(CONV_W):
            lag = CONV_W - 1 - t
            out = out + cw_ref[t:t + 1, cs] * ext_s[CONV_PAD - lag:CONV_PAD - lag + tm, cs]
        return out

    subs = [slice(s * FF_SUB, (s + 1) * FF_SUB) for s in range(FF_TILE // FF_SUB)]
    for cs in subs:
        extg_s[CONV_PAD:, cs] = jnp.dot(h_s[...], wug_ref[:, cs], preferred_element_type=F32)
    for cs in subs:
        extv_s[CONV_PAD:, cs] = jnp.dot(h_s[...], wuv_ref[:, cs], preferred_element_type=F32)
    gates = [jax.nn.silu(conv(extg_s, cwg_ref, cbg_ref, cs)) for cs in subs]
    down = None
    for gate, cs in zip(gates, subs):
        act = (gate * conv(extv_s, cwv_ref, cbv_ref, cs)).astype(BF16)
        part = jnp.dot(act, wd_ref[cs, :], preferred_element_type=F32)
        down = part if down is None else down + part
    o_ref[...] += down

    carryg_s[f] = extg_s[tm:tm + CONV_PAD, :]
    carryv_s[f] = extv_s[tm:tm + CONV_PAD, :]
    cg_ref[f] = extg_s[tm + CONV_PAD - (CONV_W - 1):tm + CONV_PAD, :]
    cv_ref[f] = extv_s[tm + CONV_PAD - (CONV_W - 1):tm + CONV_PAD, :]

    if final_norm:
        @pl.when(f == pl.num_programs(1) - 1)
        def _():
            o_ref[...] = _rms(o_ref[...], fg_ref[...])


def _ffn_prompt(l, x, gn, wug, wuv, wd, cw, cb, fg, batch, seq, tm, final_norm):
    m = batch * seq
    tiles_per_seq = seq // tm
    row = lambda i, f: (i, 0)
    state = lambda i, f: (i // tiles_per_seq, 0, 0, 0)
    return pl.pallas_call(
        functools.partial(_ffn_prompt_kernel, final_norm, tiles_per_seq),
        grid=(m // tm, N_FF),
        in_specs=[pl.BlockSpec((tm, D_MODEL), row),
                  _of_layer(l, (1, D_MODEL)), _resident((1, D_MODEL))]
                 + _ffn_cast_weight_specs(lambda i, f: f) + _ffn_conv_specs(l, lambda i, f: f),
        out_specs=[pl.BlockSpec((tm, D_MODEL), row),
                   pl.BlockSpec((None, N_FF, CONV_W - 1, FF_TILE), state),
                   pl.BlockSpec((None, N_FF, CONV_W - 1, FF_TILE), state)],
        out_shape=[jax.ShapeDtypeStruct((m, D_MODEL), F32),
                   jax.ShapeDtypeStruct((batch, N_FF, CONV_W - 1, FF_TILE), F32),
                   jax.ShapeDtypeStruct((batch, N_FF, CONV_W - 1, FF_TILE), F32)],
        scratch_shapes=[pltpu.VMEM((tm, D_MODEL), BF16),
                        pltpu.VMEM((tm + CONV_PAD, FF_TILE), F32),
                        pltpu.VMEM((tm + CONV_PAD, FF_TILE), F32),
                        pltpu.VMEM((N_FF, CONV_PAD, FF_TILE), F32),
                        pltpu.VMEM((N_FF, CONV_PAD, FF_TILE), F32)],
        compiler_params=_params(("arbitrary", "arbitrary"), 62),
        name="ffn_prompt",
    )(x, gn, fg, wug, wuv, wd, cw, cw, cb, cb)


CAST_ROWS = 256


def _cast_weight(w_ref, wb_ref):
    for r in range(0, w_ref.shape[0], CAST_ROWS):
        wb_ref[r:r + CAST_ROWS, :] = w_ref[r:r + CAST_ROWS, :].astype(BF16)


def _ffn_sample_kernel(final_norm,
                       x_ref, gn_ref, fg_ref, wug_ref, wuv_ref, wd_ref, cwg_ref, cwv_ref, cbg_ref, cbv_ref,
                       s0g_ref, s0v_ref, s1g_ref, s1v_ref,
                       o_ref, ug_ref, uv_ref, wugb_ref, wuvb_ref, wdb_ref, h_s):
    f = pl.program_id(0)

    @pl.when(f == 0)
    def _():
        x = x_ref[...]
        h_s[...] = _rms(x, gn_ref[...]).astype(BF16)
        o_ref[...] = x

    def conv(w_ref, wb_ref, cw_ref, cb_ref, s0_ref, s1_ref, up_ref):
        _cast_weight(w_ref, wb_ref)
        up = jnp.dot(h_s[...], wb_ref[...], preferred_element_type=F32)
        up_ref[...] = up
        return ((cb_ref[...] + cw_ref[0:1, :] * s0_ref[...]) + cw_ref[1:2, :] * s1_ref[...]) + cw_ref[2:3, :] * up

    gate = conv(wug_ref, wugb_ref, cwg_ref, cbg_ref, s0g_ref, s1g_ref, ug_ref)
    val = conv(wuv_ref, wuvb_ref, cwv_ref, cbv_ref, s0v_ref, s1v_ref, uv_ref)
    act = (jax.nn.silu(gate) * val).astype(BF16)
    _cast_weight(wd_ref, wdb_ref)
    o_ref[...] += jnp.dot(act, wdb_ref[...], preferred_element_type=F32)

    if final_norm:
        @pl.when(f == pl.num_programs(0) - 1)
        def _():
            o_ref[...] = _rms(o_ref[...], fg_ref[...])


def _ffn_sample(l, x, gn, wu, wd, cw, cb, fg, s0, s1, final_norm):
    n = x.shape[0]
    gcol = lambda f: (l, 0, f)
    vcol = lambda f: (l, 0, N_FF + f)
    st = lambda im: pl.BlockSpec((None, n, FF_TILE), im)
    return pl.pallas_call(
        functools.partial(_ffn_sample_kernel, final_norm),
        grid=(N_FF,),
        in_specs=[_resident((n, D_MODEL)), _of_layer(l, (1, D_MODEL)), _resident((1, D_MODEL)),
                  pl.BlockSpec((None, D_MODEL, FF_TILE), gcol), pl.BlockSpec((None, D_MODEL, FF_TILE), vcol),
                  pl.BlockSpec((None, FF_TILE, D_MODEL), lambda f: (l, f, 0))]
                 + _ffn_conv_specs(l, lambda f: f)
                 + [st(gcol), st(vcol), st(gcol), st(vcol)],
        out_specs=[pl.BlockSpec((n, D_MODEL), lambda f: (0, 0)),
                   pl.BlockSpec((n, FF_TILE), lambda f: (0, f)),
                   pl.BlockSpec((n, FF_TILE), lambda f: (0, f))]
                  + _ffn_cast_weight_specs(lambda f: f),
        out_shape=[jax.ShapeDtypeStruct((n, D_MODEL), F32),
                   jax.ShapeDtypeStruct((n, D_FF), F32),
                   jax.ShapeDtypeStruct((n, D_FF), F32),
                   jax.ShapeDtypeStruct((D_MODEL, D_FF), BF16),
                   jax.ShapeDtypeStruct((D_MODEL, D_FF), BF16),
                   jax.ShapeDtypeStruct((D_FF, D_MODEL), BF16)],
        scratch_shapes=[pltpu.VMEM((n, D_MODEL), BF16)],
        compiler_params=_params(("arbitrary",), 44),
        name="ffn_sample",
    )(x, gn, fg, wu, wu, wd, cw, cw, cb, cb, s0, s0, s1, s1)


def _rope_tables(pos):
    half = HEAD_DIM // 2
    inv = ROPE_THETA ** (-jnp.arange(half, dtype=F32) / half)
    ang = pos.astype(F32)[:, None] * inv[None, :]
    cos, sin = jnp.cos(ang), jnp.sin(ang)
    reps = LANES // HEAD_DIM
    return (jnp.concatenate([cos, cos] * reps, axis=1),
            jnp.concatenate([-sin, sin] * reps, axis=1))


def _ssm_params(a_re, a_im, b_re, b_im, c_re, c_im, d, log_dt):
    depth = a_re.shape[0]
    dt = jnp.exp(log_dt)[..., None]
    mag = jnp.exp(a_re * dt)
    lr, li = mag * jnp.cos(a_im * dt), mag * jnp.sin(a_im * dt)
    nr, ni = lr - 1.0, li
    den = a_re * a_re + a_im * a_im
    qr = (nr * a_re + ni * a_im) / den
    qi = (ni * a_re - nr * a_im) / den
    bbr = qr[..., None] * b_re - qi[..., None] * b_im
    bbi = qr[..., None] * b_im + qi[..., None] * b_re

    def compact(t):
        r, c = t.shape[2], t.shape[3]
        t = t.reshape(depth, N_CHUNK, GROUP_CHUNK * r, c).astype(BF16)
        return jnp.pad(t, ((0, 0), (0, 0), (0, 0), (0, LANES - c)))

    def lane_tiling(c):
        src = jnp.arange(LANES)[:, None]
        dst = jnp.arange(GROUP_CHUNK * c)[None, :] % c
        return (src == dst).astype(BF16)

    return dict(bre=compact(bbr.transpose(0, 1, 3, 2)), bim=compact(bbi.transpose(0, 1, 3, 2)),
                cre=compact(c_re.transpose(0, 1, 3, 2)), cimn=compact((-c_im).transpose(0, 1, 3, 2)),
                tile_b=lane_tiling(SSM_STATE), tile_c=lane_tiling(SSM_GROUP),
                lre=lr.reshape(depth, 1, N_STATE), lim=li.reshape(depth, 1, N_STATE),
                d=d.reshape(depth, 1, SSM_WIDTH))


def _segment_permutation():
    r = jnp.arange(SSM_TILE)
    src = (r % SUBLANES) * SEG_LEN + r // SUBLANES
    return (src[:, None] == jnp.arange(SSM_TILE)[None, :]).astype(BF16)


def kernel(x_prompt, x_sample, cache_k, cache_v, state_ssm_re, state_ssm_im, state_conv, attn_norm_g, w_in, attn_sinks, ssm_a_re, ssm_a_im, ssm_b_re, ssm_b_im, ssm_c_re, ssm_c_im, ssm_d, ssm_log_dt, w_glu, b_glu, attn_out_norm_g, ssm_out_norm_g, w_out, ffn_norm_g, w_up, conv_w, conv_b, w_down, final_norm_g):
    batch, seq, _ = x_prompt.shape
    nsamp, dec_seq, _ = x_sample.shape
    wbuf = cache_k.shape[2]
    assert dec_seq == 1 and wbuf == WINDOW and seq % (SSM_SUBTILES * SSM_TILE) == 0
    assert PAST_LEN >= wbuf

    w_in_b, w_glu_b, w_out_b = w_in.astype(BF16), w_glu.astype(BF16), w_out.astype(BF16)
    ssm = _ssm_params(ssm_a_re, ssm_a_im, ssm_b_re, ssm_b_im, ssm_c_re, ssm_c_im, ssm_d, ssm_log_dt)
    perm = _segment_permutation()

    cos_p, sin_p = _rope_tables(jnp.arange(seq, dtype=jnp.int32))
    cos_s, sin_s = _rope_tables(jnp.full((nsamp,), PAST_LEN, dtype=jnp.int32))

    rows = lambda a: a.reshape(a.shape[0], 1, a.shape[-1])
    g_in, g_a, g_s, g_f = rows(attn_norm_g), rows(attn_out_norm_g), rows(ssm_out_norm_g), rows(ffn_norm_g)
    bg, cb = rows(b_glu), rows(conv_b)
    fg = final_norm_g.reshape(1, D_MODEL)
    sink_col = attn_sinks.reshape(DEPTH, N_HEADS, 1)
    head_is_lo = (jnp.arange(N_HEADS) < KV_REP)[None, :, None]
    kc = cache_k.reshape(DEPTH, nsamp, wbuf, KV_COLS)
    vc = cache_v.reshape(DEPTH, nsamp, wbuf, KV_COLS)
    h0re = state_ssm_re.reshape(DEPTH, nsamp, N_STATE)
    h0im = state_ssm_im.reshape(DEPTH, nsamp, N_STATE)
    conv0, conv1 = state_conv[:, :, 0, :], state_conv[:, :, 1, :]

    xp = x_prompt.reshape(batch * seq, D_MODEL)
    xs = x_sample.reshape(nsamp, D_MODEL)
    tm_p = 1024
    tm_ffn = 1024
    outs = {k: [] for k in ("kp", "vp", "hrp", "hip", "cgp", "cvp", "ks", "vs", "hrs", "his", "ugs", "uvs")}

    for l in range(DEPTH):
        last = l == DEPTH - 1

        q, k, v, u = _in_proj(l, xs, g_in, w_in_b, cos_s, sin_s, nsamp)
        qh = q.reshape(nsamp, N_HEADS, HEAD_DIM)
        zq = jnp.zeros_like(qh)
        qbd = jnp.where(head_is_lo, jnp.concatenate([qh, zq], axis=-1), jnp.concatenate([zq, qh], axis=-1))
        o = _attn_sample(l, qbd, kc, vc, k.reshape(nsamp, 1, KV_COLS), v.reshape(nsamp, 1, KV_COLS), sink_col, 8)
        a_raw = jnp.concatenate([o[:, :KV_REP, :HEAD_DIM].reshape(nsamp, -1),
                                 o[:, KV_REP:, HEAD_DIM:].reshape(nsamp, -1)], axis=1)
        a = _rms_cast(l, a_raw, g_a)
        s, hre, him = _ssm_sample(l, u, h0re, h0im, ssm, w_glu_b, bg, g_s)
        x1 = _out_proj(l, xs, a, s, w_out_b, nsamp)
        xs, ug, uv, wug_b, wuv_b, wd_b = _ffn_sample(l, x1, g_f, w_up, w_down, conv_w, cb, fg, conv0, conv1, last)
        outs["ks"].append(k)
        outs["vs"].append(v)
        outs["hrs"].append(hre)
        outs["his"].append(him)
        outs["ugs"].append(ug)
        outs["uvs"].append(uv)

        q, k, v, u = _in_proj(l, xp, g_in, w_in_b, cos_p, sin_p, tm_p)
        a = _attn_prompt(l, q, k, v, attn_sinks, g_a, batch, seq)
        s, hre, him = _ssm_prompt(l, u, perm, ssm, w_glu_b, bg, g_s, batch, seq)
        x1 = _out_proj(l, xp, a, s, w_out_b, tm_p)
        xp, cg, cv = _ffn_prompt(l, x1, g_f, wug_b, wuv_b, wd_b, conv_w, cb, fg, batch, seq, tm_ffn, last)
        outs["kp"].append(k.reshape(batch, seq, KV_COLS)[:, seq - WINDOW:])
        outs["vp"].append(v.reshape(batch, seq, KV_COLS)[:, seq - WINDOW:])
        outs["hrp"].append(hre)
        outs["hip"].append(him)
        outs["cgp"].append(cg)
        outs["cvp"].append(cv)

    st = lambda name: jnp.stack(outs[name], axis=0)
    heads = lambda t: t.reshape(t.shape[:-1] + (N_KV_HEADS, HEAD_DIM))
    states = lambda t: t.reshape(DEPTH, -1, SSM_GROUPS, SSM_STATE)
    unblock = lambda t: t.transpose(0, 1, 3, 2, 4).reshape(DEPTH, batch, CONV_W - 1, D_FF)
    conv_prompt = jnp.concatenate([unblock(st("cgp")), unblock(st("cvp"))], axis=-1)
    k_sample = jnp.concatenate([cache_k[:, :, 1:], heads(st("ks"))[:, :, None]], axis=2)
    v_sample = jnp.concatenate([cache_v[:, :, 1:], heads(st("vs"))[:, :, None]], axis=2)
    conv_sample = jnp.stack([conv1, jnp.concatenate([st("ugs"), st("uvs")], axis=-1)], axis=2)
    return (xp.reshape(batch, seq, D_MODEL), xs.reshape(nsamp, 1, D_MODEL),
            heads(st("kp")), heads(st("vp")), states(st("hrp")), states(st("hip")), conv_prompt,
            k_sample, v_sample, states(st("hrs")), states(st("his")), conv_sample)
```
